```python
import jax, jax.numpy as jnp
from jax import lax
import numpy as np

D_MODEL = 1024
BATCH = 2
SEQ = 16384
DEPTH = 1

CTX_LEN = 256
GRID_W = 64
EPS = 1e-6
N_MOD = 9
D_FF = 2816
M_HEADS = 4
M_HD = 128
M_W = M_HEADS * M_HD
CONV_W = 3
CHUNK = 128
POOL_WINDOWS = (2, 4, 8, 16)
POOL_GROUPS = len(POOL_WINDOWS)
POOL_W = 512
POOL_GW = POOL_W // POOL_GROUPS
MIX_W = M_W + POOL_W
M_COLS = 3 * M_W + 4 * M_HEADS
IN_COLS = M_COLS + POOL_W

kernel_name = 'hybrid_mlstm_pool_macaron_dit'


def rmsnorm(x, g):
    xf = x.astype(jnp.float32)
    y = xf * lax.rsqrt(jnp.mean(xf * xf, axis=-1, keepdims=True) + EPS)
    return (y * g.astype(jnp.float32)).astype(x.dtype)


def ada_in(x, mod, j, g):
    return rmsnorm(x, g) * (1 + mod[:, 3 * j + 1, None]) + mod[:, 3 * j, None]


def ada_out(x, z, mod, j, g, res_w):
    return x + res_w * mod[:, 3 * j + 2, None] * rmsnorm(z, g)


def swiglu(y, w_in, w_out):
    a, b = jnp.split(y @ w_in, 2, axis=-1)
    return (jax.nn.silu(a) * b) @ w_out


def centred_mean(u, win, axis):
    n = u.shape[axis]
    lo, hi = win // 2, win - 1 - win // 2
    pad = [(0, 0)] * u.ndim
    pad[axis] = (lo + 1, hi)
    s = jnp.cumsum(jnp.pad(u.astype(jnp.float32), pad), axis=axis)
    tot = lax.slice_in_dim(s, win, win + n, axis=axis) - lax.slice_in_dim(s, 0, n, axis=axis)
    t = jnp.arange(n)
    cnt = (jnp.minimum(t + hi, n - 1) - jnp.maximum(t - lo, 0) + 1).astype(jnp.float32)
    shape = [1] * u.ndim
    shape[axis] = n
    return (tot / cnt.reshape(shape)).astype(u.dtype)


def pool_mixer(u, w_pool, pool_scale, on_grid):
    bsz, n, _ = u.shape
    outs = []
    for gi, win in enumerate(POOL_WINDOWS):
        ug = u[..., gi * POOL_GW:(gi + 1) * POOL_GW]
        if on_grid:
            rows = n // GRID_W
            ug2 = ug.reshape(bsz, rows, GRID_W, POOL_GW)
            pg = centred_mean(centred_mean(ug2, win, 2), win, 1).reshape(bsz, n, POOL_GW)
        else:
            pg = centred_mean(ug, win, 1)
        outs.append(pg - ug)
    d = jnp.stack(outs, axis=2)
    y = jnp.einsum('bngc,gcd->bngd', d, w_pool).reshape(bsz, n, POOL_W)
    return y * pool_scale


def short_conv(u, w, b):
    n = u.shape[1]
    h = CONV_W // 2
    up = jnp.pad(u, ((0, 0), (h, h), (0, 0)))
    out = b
    for j in range(CONV_W):
        out = out + up[:, j:j + n] * w[j]
    return out


def mlstm_inputs(p, conv_w, conv_b, w_q, w_k, i_bias, f_bias):
    bsz, n, _ = p.shape
    u = jax.nn.silu(short_conv(p[..., :M_W], conv_w, conv_b)).reshape(bsz, n, M_HEADS, M_HD)
    q = jnp.einsum('bnhd,hde->bhne', u, w_q).astype(jnp.float32)
    k = jnp.einsum('bnhd,hde->bhne', u, w_k).astype(jnp.float32)
    v = p[..., M_W:2 * M_W].reshape(bsz, n, M_HEADS, M_HD).transpose(0, 2, 1, 3).astype(jnp.float32)
    g0 = 3 * M_W
    ig = (p[..., g0:g0 + 2 * M_HEADS].reshape(bsz, n, 2, M_HEADS) + i_bias).astype(jnp.float32)
    fg = (p[..., g0 + 2 * M_HEADS:M_COLS].reshape(bsz, n, 2, M_HEADS) + f_bias).astype(jnp.float32)
    return q, k, v, ig.transpose(2, 0, 3, 1), fg.transpose(2, 0, 3, 1)


def mlstm_chunkwise(q, k, v, i_pre, f_pre):
    bsz, nh, t_len, dk = q.shape
    nc = t_len // CHUNK
    q = q.reshape(bsz, nh, nc, CHUNK, dk)
    k = k.reshape(bsz, nh, nc, CHUNK, dk) * (dk ** -0.5)
    v = v.reshape(bsz, nh, nc, CHUNK, dk)
    ip = i_pre.reshape(bsz, nh, nc, CHUNK)
    b = jnp.cumsum(jax.nn.log_sigmoid(f_pre).reshape(bsz, nh, nc, CHUNK), axis=-1)
    g = b[..., -1]
    a = g[..., None] - b + ip
    m_loc = jnp.max(a, axis=-1)
    wk = jnp.exp(a - m_loc[..., None])[..., None] * k
    ckv = jnp.einsum('bhclk,bhclv->bhckv', wk, v)
    cn = jnp.sum(wk, axis=3)

    def step(carry, inp):
        c_st, n_st, m_st = carry
        g_c, m_c, ckv_c, cn_c = inp
        m_new = jnp.maximum(g_c + m_st, m_c)
        a_old = jnp.exp(g_c + m_st - m_new)
        a_new = jnp.exp(m_c - m_new)
        c_new = a_old[..., None, None] * c_st + a_new[..., None, None] * ckv_c
        n_new = a_old[..., None] * n_st + a_new[..., None] * cn_c
        return (c_new, n_new, m_new), (c_st, n_st, m_st)

    init = (jnp.zeros((bsz, nh, dk, dk), jnp.float32), jnp.zeros((bsz, nh, dk), jnp.float32),
            jnp.zeros((bsz, nh), jnp.float32))
    xs = (jnp.moveaxis(g, -1, 0), jnp.moveaxis(m_loc, -1, 0), jnp.moveaxis(ckv, 2, 0), jnp.moveaxis(cn, 2, 0))
    _, (c_in, n_in, m_in) = lax.scan(step, init, xs)
    c_in = jnp.moveaxis(c_in, 0, 2)
    n_in = jnp.moveaxis(n_in, 0, 2)
    m_in = jnp.moveaxis(m_in, 0, -1)
    mask = jnp.tril(jnp.ones((CHUNK, CHUNK), dtype=bool))
    dmat = jnp.where(mask, b[..., :, None] - b[..., None, :] + ip[..., None, :], -jnp.inf)
    inter = b + m_in[..., None]
    m_t = jnp.maximum(jnp.max(dmat, axis=-1), inter)
    s = jnp.einsum('bhctk,bhcsk->bhcts', q, k) * jnp.exp(dmat - m_t[..., None])
    w_inter = jnp.exp(inter - m_t)
    num = w_inter[..., None] * jnp.einsum('bhctk,bhckv->bhctv', q, c_in) + jnp.einsum('bhcts,bhcsv->bhctv', s, v)
    den = w_inter * jnp.einsum('bhctk,bhck->bhct', q, n_in) + jnp.sum(s, axis=-1)
    h = num / jnp.maximum(jnp.abs(den), jnp.exp(-m_t))[..., None]
    return h.reshape(bsz, nh, t_len, dk)


def mlstm_bidir(p_lat, p_ctx, conv_w, conv_b, w_q, w_k, i_bias, f_bias):
    ql, kl, vl, il, fl = mlstm_inputs(p_lat, conv_w, conv_b, w_q, w_k, i_bias, f_bias)
    qc, kc, vc, ic, fc = mlstm_inputs(p_ctx, conv_w, conv_b, w_q, w_k, i_bias, f_bias)
    lc = qc.shape[2]
    cat = lambda a, b: jnp.concatenate([a, b], axis=2)
    rev = lambda a: jnp.flip(a, axis=2)
    h_f = mlstm_chunkwise(cat(qc, ql), cat(kc, kl), cat(vc, vl), cat(ic[0], il[0]), cat(fc[0], fl[0]))
    h_b = mlstm_chunkwise(cat(rev(qc), rev(ql)), cat(rev(kc), rev(kl)), cat(rev(vc), rev(vl)),
                          cat(rev(ic[1]), rev(il[1])), cat(rev(fc[1]), rev(fl[1])))
    h_lat = h_f[:, :, lc:] + rev(h_b[:, :, lc:])
    h_ctx = h_f[:, :, :lc] + rev(h_b[:, :, :lc])
    return h_lat, h_ctx


def mlstm_out(h, o, g):
    bsz, _, n, _ = h.shape
    h = h.transpose(0, 2, 1, 3)
    h = h * lax.rsqrt(jnp.mean(h * h, axis=-1, keepdims=True) + EPS) * g.reshape(M_HEADS, M_HD).astype(jnp.float32)
    return (h.reshape(bsz, n, M_W) * jax.nn.sigmoid(o.astype(jnp.float32))).astype(o.dtype)


def setup_inputs(seed: int = 0) -> dict:
    key = jax.random.key(seed)
    ks = jax.random.split(key, 24)
    nrm = lambda k, shape, s: jax.random.normal(k, shape, jnp.float32) * s
    f_base = jnp.linspace(3.0, 6.0, M_HEADS, dtype=jnp.float32)
    return {
        'x': nrm(ks[0], (BATCH, SEQ, D_MODEL), 1.0),
        'c': nrm(ks[1], (BATCH, D_MODEL), 1.0),
        'ctx': nrm(ks[2], (BATCH, CTX_LEN, D_MODEL), 1.0),
        'c_ctx': nrm(ks[3], (D_MODEL,), 1.0),
        'w_mod': nrm(ks[4], (DEPTH, D_MODEL, N_MOD * D_MODEL), D_MODEL ** -0.5),
        'b_mod': nrm(ks[5], (DEPTH, N_MOD * D_MODEL), 0.02),
        'norm_pre': 1.0 + nrm(ks[6], (DEPTH, 3, D_MODEL), 0.05),
        'norm_post': 1.0 + nrm(ks[7], (DEPTH, 3, D_MODEL), 0.05),
        'ffn_w_in': nrm(ks[8], (DEPTH, 2, D_MODEL, 2 * D_FF), D_MODEL ** -0.5),
        'ffn_w_out': nrm(ks[9], (DEPTH, 2, D_FF, D_MODEL), D_FF ** -0.5),
        'w_in': nrm(ks[10], (DEPTH, D_MODEL, IN_COLS), D_MODEL ** -0.5),
        'w_out': nrm(ks[11], (DEPTH, MIX_W, D_MODEL), MIX_W ** -0.5),
        'conv_w': nrm(ks[12], (DEPTH, CONV_W, M_W), CONV_W ** -0.5),
        'conv_b': nrm(ks[13], (DEPTH, M_W), 0.02),
        'w_q': nrm(ks[14], (DEPTH, M_HEADS, M_HD, M_HD), M_HD ** -0.5),
        'w_k': nrm(ks[15], (DEPTH, M_HEADS, M_HD, M_HD), M_HD ** -0.5),
        'i_bias': nrm(ks[16], (DEPTH, 2, M_HEADS), 0.1),
        'f_bias': f_base + nrm(ks[17], (DEPTH, 2, M_HEADS), 0.1),
        'head_norm': 1.0 + nrm(ks[18], (DEPTH, M_W), 0.05),
        'pool_w': nrm(ks[19], (DEPTH, POOL_GROUPS, POOL_GW, POOL_GW), POOL_GW ** -0.5),
        'pool_scale': 1.0 + nrm(ks[20], (DEPTH, POOL_W), 0.1),
    }


def reference(x, c, ctx, c_ctx, w_mod, b_mod, norm_pre, norm_post, ffn_w_in, ffn_w_out, w_in, w_out,
              conv_w, conv_b, w_q, w_k, i_bias, f_bias, head_norm, pool_w, pool_scale):
    sc = jax.nn.silu(c)
    scc = jax.nn.silu(c_ctx)[None]
    xc = ctx
    for l in range(DEPTH):
        ctx_out = l < DEPTH - 1
        mod = (sc @ w_mod[l] + b_mod[l]).reshape(-1, N_MOD, D_MODEL)
        modc = (scc @ w_mod[l] + b_mod[l]).reshape(1, N_MOD, D_MODEL)
        x = ada_out(x, swiglu(ada_in(x, mod, 0, norm_pre[l, 0]), ffn_w_in[l, 0], ffn_w_out[l, 0]),
                    mod, 0, norm_post[l, 0], 0.5)
        xc = ada_out(xc, swiglu(ada_in(xc, modc, 0, norm_pre[l, 0]), ffn_w_in[l, 0], ffn_w_out[l, 0]),
                     modc, 0, norm_post[l, 0], 0.5)
        p = ada_in(x, mod, 1, norm_pre[l, 1]) @ w_in[l]
        pc = ada_in(xc, modc, 1, norm_pre[l, 1]) @ (w_in[l] if ctx_out else w_in[l, :, :M_COLS])
        h_lat, h_ctx = mlstm_bidir(p[..., :M_COLS], pc[..., :M_COLS], conv_w[l], conv_b[l],
                                   w_q[l], w_k[l], i_bias[l], f_bias[l])
        mix = jnp.concatenate([mlstm_out(h_lat, p[..., 2 * M_W:3 * M_W], head_norm[l]),
                               pool_mixer(p[..., M_COLS:], pool_w[l], pool_scale[l], True)], axis=-1)
        x = ada_out(x, mix @ w_out[l], mod, 1, norm_post[l, 1], 1.0)
        if ctx_out:
            mixc = jnp.concatenate([mlstm_out(h_ctx, pc[..., 2 * M_W:3 * M_W], head_norm[l]),
                                    pool_mixer(pc[..., M_COLS:], pool_w[l], pool_scale[l], False)], axis=-1)
            xc = ada_out(xc, mixc @ w_out[l], modc, 1, norm_post[l, 1], 1.0)
        x = ada_out(x, swiglu(ada_in(x, mod, 2, norm_pre[l, 2]), ffn_w_in[l, 1], ffn_w_out[l, 1]),
                    mod, 2, norm_post[l, 2], 0.5)
        if ctx_out:
            xc = ada_out(xc, swiglu(ada_in(xc, modc, 2, norm_pre[l, 2]), ffn_w_in[l, 1], ffn_w_out[l, 1]),
                         modc, 2, norm_post[l, 2], 0.5)
    return x
```

```python
import functools

import jax
import jax.numpy as jnp
from jax import lax
from jax.experimental import pallas as pl
from jax.experimental.pallas import tpu as pltpu

F32 = jnp.float32
BF16 = jnp.bfloat16

EPS = 1e-6
N_MOD = 9
HEADS = 4
HEAD_DIM = 128
M_W = HEADS * HEAD_DIM
POOL_WINDOWS = (2, 4, 8, 16)
POOL_GW = 128
POOL_W = POOL_GW * len(POOL_WINDOWS)
GRID_W = 64
CHUNK = 128
GATE_COLS = 4 * HEADS
GATE_PAD = 128
FF_CHUNK = 256
V7X_VMEM_LIMIT = 56 * 1024 * 1024

_NT = (((1,), (1,)), ((), ()))


def _sigmoid(x):
    return 1.0 / (1.0 + jnp.exp(-x))


def _log_sigmoid(x):
    return jnp.minimum(x, 0.0) - jnp.log(1.0 + jnp.exp(-jnp.abs(x)))


def _rms(x, g):
    ms = jnp.mean(x * x, axis=-1, keepdims=True)
    return x * lax.rsqrt(ms + EPS) * g


def _dot(a, b):
    return jnp.dot(a, b, preferred_element_type=F32)


def _const_spec(shape):
    nd = len(shape)
    return pl.BlockSpec(shape, lambda *_: (0,) * nd, pipeline_mode=pl.Buffered(1))


def _params(sem):
    return pltpu.CompilerParams(dimension_semantics=sem, vmem_limit_bytes=V7X_VMEM_LIMIT)


def _mod_kernel(s_ref, w_ref, b_ref, o_ref):
    s = s_ref[...]
    act = (s * _sigmoid(s)).astype(BF16)
    o_ref[...] = _dot(act, w_ref[...].astype(BF16)) + b_ref[...]


def _modulation(s_rows, w_mod, b_mod):
    rows, d = s_rows.shape
    n = w_mod.shape[1]
    tn = 1024
    return pl.pallas_call(
        _mod_kernel,
        grid=(n // tn,),
        in_specs=[pl.BlockSpec((rows, d), lambda j: (0, 0)),
                  pl.BlockSpec((d, tn), lambda j: (0, j)),
                  pl.BlockSpec((1, tn), lambda j: (0, j))],
        out_specs=pl.BlockSpec((rows, tn), lambda j: (0, j)),
        out_shape=jax.ShapeDtypeStruct((rows, n), F32),
        compiler_params=_params(("arbitrary",)),
        name="mod",
    )(s_rows, w_mod, b_mod.reshape(1, n))


def _swiglu_into(acc_ref, y_ref, w1a_ref, w1b_ref, w2_ref):
    y = y_ref[...]
    for j in range(w1a_ref.shape[0]):
        a = _dot(y, w1a_ref[j])
        b = _dot(y, w1b_ref[j])
        h = ((a * _sigmoid(a)) * b).astype(BF16)
        part = _dot(h, w2_ref[j])
        if j == 0:
            acc_ref[...] = part
        else:
            acc_ref[...] += part


def _ffn_in_kernel(with_rest, x_ref, mod_ref, npre_ref, npost_ref, w1a_ref, w1b_ref, w2_ref,
                   win_ref, gbias_ref, *rest):
    if with_rest:
        x1_ref, qk_ref, v_ref, g_ref, o_ref, pool_ref, y_scr, acc_scr = rest
    else:
        qk_ref, v_ref, g_ref, y_scr, acc_scr = rest
    x = x_ref[...]
    y = _rms(x, npre_ref[0:1, :]) * (1.0 + mod_ref[0, 1:2, :]) + mod_ref[0, 0:1, :]
    y_scr[...] = y.astype(BF16)
    _swiglu_into(acc_scr, y_scr, w1a_ref, w1b_ref, w2_ref)
    x1 = x + (0.5 * mod_ref[0, 2:3, :]) * _rms(acc_scr[...], npost_ref[0:1, :])
    y2 = _rms(x1, npre_ref[1:2, :]) * (1.0 + mod_ref[0, 4:5, :]) + mod_ref[0, 3:4, :]
    y_scr[...] = y2.astype(BF16)
    y2b = y_scr[...]
    c0, c1, c2 = M_W, 2 * M_W, 2 * M_W + GATE_PAD
    qk_ref[...] = _dot(y2b, win_ref[:, 0:c0])
    v_ref[...] = _dot(y2b, win_ref[:, c0:c1]).astype(BF16)
    g_ref[...] = _dot(y2b, win_ref[:, c1:c2]) + gbias_ref[...]
    if with_rest:
        c3 = c2 + M_W
        x1_ref[...] = x1
        o_ref[...] = _dot(y2b, win_ref[:, c2:c3])
        pool_ref[...] = _dot(y2b, win_ref[:, c3:c3 + POOL_W])


def _ffn_in(x, mod_all, mod_row0, npre, npost, w1a, w1b, w2, win, gbias, *, with_rest, tm):
    bsz, n, d = x.shape
    grid = (bsz, n // tm)
    tok = lambda w: pl.BlockSpec((None, tm, w), lambda b, i: (b, i, 0))
    if mod_row0 is None:
        mod_spec = pl.BlockSpec((1, N_MOD, d), lambda b, i: (b, 0, 0))
    else:
        mod_spec = pl.BlockSpec((1, N_MOD, d), lambda b, i: (mod_row0, 0, 0))
    in_specs = [tok(d), mod_spec, _const_spec(npre.shape), _const_spec(npost.shape),
                _const_spec(w1a.shape), _const_spec(w1b.shape), _const_spec(w2.shape),
                _const_spec(win.shape), _const_spec(gbias.shape)]
    sds = lambda w, dt: jax.ShapeDtypeStruct((bsz, n, w), dt)
    out_specs = [tok(M_W), tok(M_W), tok(GATE_PAD)]
    out_shape = [sds(M_W, F32), sds(M_W, BF16), sds(GATE_PAD, F32)]
    if with_rest:
        out_specs = [tok(d)] + out_specs + [tok(M_W), tok(POOL_W)]
        out_shape = [sds(d, F32)] + out_shape + [sds(M_W, F32), sds(POOL_W, F32)]
    return pl.pallas_call(
        functools.partial(_ffn_in_kernel, with_rest),
        grid=grid,
        in_specs=in_specs,
        out_specs=out_specs,
        out_shape=out_shape,
        scratch_shapes=[pltpu.VMEM((tm, d), BF16), pltpu.VMEM((tm, d), F32)],
        compiler_params=_params(("parallel", "arbitrary")),
        name="ffn_in" if with_rest else "ffn_in_ctx",
    )(x, mod_all, npre, npost, w1a, w1b, w2, win, gbias)


def _qk_kernel(nt, x_ref, prev_ref, next_ref, cw_ref, cb_ref, wq_ref, wk_ref, q_ref, k_ref):
    j = pl.program_id(1)
    x = x_ref[...]
    tq = x.shape[0]
    prev_row = prev_ref[7:8, :] * jnp.where(j > 0, 1.0, 0.0)
    next_row = next_ref[0:1, :] * jnp.where(j < nt - 1, 1.0, 0.0)
    row = lax.broadcasted_iota(jnp.int32, x.shape, 0)
    up = jnp.where(row == 0, prev_row, pltpu.roll(x, 1, 0))
    dn = jnp.where(row == tq - 1, next_row, pltpu.roll(x, tq - 1, 0))
    conv = cb_ref[...] + up * cw_ref[0:1, :] + x * cw_ref[1:2, :] + dn * cw_ref[2:3, :]
    u = (conv * _sigmoid(conv)).astype(BF16)
    scale = HEAD_DIM ** -0.5
    for h in range(HEADS):
        sl = slice(h * HEAD_DIM, (h + 1) * HEAD_DIM)
        uh = u[:, sl]
        q_ref[:, sl] = _dot(uh, wq_ref[h]).astype(BF16)
        k_ref[:, sl] = (_dot(uh, wk_ref[h]) * scale).astype(BF16)


def _qk_proj(qk_src, conv_w, conv_b, wq, wk, *, tq):
    bsz, n, w = qk_src.shape
    nt = n // tq
    blk8 = tq // 8
    last8 = n // 8 - 1
    tok = pl.BlockSpec((None, tq, w), lambda b, j: (b, j, 0))
    prev = pl.BlockSpec((None, 8, w), lambda b, j: (b, jnp.maximum(j * blk8 - 1, 0), 0))
    nxt = pl.BlockSpec((None, 8, w), lambda b, j: (b, jnp.minimum((j + 1) * blk8, last8), 0))
    out = jax.ShapeDtypeStruct((bsz, n, w), BF16)
    return pl.pallas_call(
        functools.partial(_qk_kernel, nt),
        grid=(bsz, nt),
        in_specs=[tok, prev, nxt, _const_spec(conv_w.shape), _const_spec(conv_b.shape),
                  _const_spec(wq.shape), _const_spec(wk.shape)],
        out_specs=[tok, tok],
        out_shape=[out, out],
        compiler_params=_params(("parallel", "arbitrary")),
        name="qk",
    )(qk_src, qk_src, qk_src, conv_w, conv_b, wq, wk)


def _pool_kernel(nt, grid_h, x_ref, prev_ref, next_ref, wp_ref, ps_ref, y_ref, xs_ref):
    j = pl.program_id(1)
    tt = x_ref.shape[0]
    halo = prev_ref.shape[0]
    rows_per_tile = tt // GRID_W
    xs_ref[0:halo, :] = prev_ref[...] * jnp.where(j > 0, 1.0, 0.0)
    xs_ref[halo:halo + tt, :] = x_ref[...]
    xs_ref[halo + tt:halo + tt + halo, :] = next_ref[...] * jnp.where(j < nt - 1, 1.0, 0.0)
    t = lax.broadcasted_iota(jnp.int32, (tt, POOL_GW), 0)
    c = jnp.bitwise_and(t, GRID_W - 1)
    r = j * rows_per_tile + jnp.right_shift(t, 6)

    def from_left(v, k):
        return jnp.where(c >= k, pltpu.roll(v, k, 0), 0.0)

    def from_right(v, k):
        return jnp.where(c < GRID_W - k, pltpu.roll(v, tt - k, 0), 0.0)

    for g, win in enumerate(POOL_WINDOWS):
        lo, hi = win // 2, win - 1 - win // 2
        sl = slice(g * POOL_GW, (g + 1) * POOL_GW)
        vs = None
        for dr in range(-lo, hi + 1):
            start = halo + dr * GRID_W
            piece = xs_ref[start:start + tt, sl]
            vs = piece if vs is None else vs + piece
        trail, lead, k = vs, vs, 1
        while k < lo:
            trail = trail + from_left(trail, k)
            lead = lead + from_right(lead, k)
            k *= 2
        tot = from_left(trail, 1) + lead
        cnt_c = jnp.minimum(c + hi, GRID_W - 1) - jnp.maximum(c - lo, 0) + 1
        cnt_r = jnp.minimum(r + hi, grid_h - 1) - jnp.maximum(r - lo, 0) + 1
        pg = tot / (cnt_c * cnt_r).astype(F32)
        dlt = (pg - x_ref[:, sl]).astype(BF16)
        y_ref[:, sl] = (_dot(dlt, wp_ref[g]) * ps_ref[:, sl]).astype(BF16)


def _pool_mixer(u, wp, ps, *, tt):
    bsz, n, w = u.shape
    nt = n // tt
    halo = 8 * GRID_W
    hb = tt // halo
    last = n // halo - 1
    tok = pl.BlockSpec((None, tt, w), lambda b, j: (b, j, 0))
    prev = pl.BlockSpec((None, halo, w), lambda b, j: (b, jnp.maximum(j * hb - 1, 0), 0))
    nxt = pl.BlockSpec((None, halo, w), lambda b, j: (b, jnp.minimum((j + 1) * hb, last), 0))
    return pl.pallas_call(
        functools.partial(_pool_kernel, nt, n // GRID_W),
        grid=(bsz, nt),
        in_specs=[tok, prev, nxt, _const_spec(wp.shape), _const_spec(ps.shape)],
        out_specs=tok,
        out_shape=jax.ShapeDtypeStruct((bsz, n, w), BF16),
        scratch_shapes=[pltpu.VMEM((tt + 2 * halo, w), F32)],
        compiler_params=_params(("parallel", "arbitrary")),
        name="pool",
    )(u, u, u, wp, ps)


def _mlstm_kernel(n_ctx_chunks, *refs):
    lat_f, ctx_f, lat_b, ctx_b = refs[0:4], refs[4:8], refs[8:12], refs[12:16]
    hf_ref, hb_ref, cn_scr, m_scr = refs[16:20]
    i = pl.program_id(1)

    @pl.when(i == 0)
    def _():
        cn_scr[...] = jnp.zeros_like(cn_scr)
        m_scr[...] = jnp.zeros_like(m_scr)

    is_ctx = i < n_ctx_chunks
    L = CHUNK
    tpos = lax.broadcasted_iota(jnp.int32, (L, L), 0)
    spos = lax.broadcasted_iota(jnp.int32, (L, L), 1)
    lane = lax.broadcasted_iota(jnp.int32, (2 * HEADS, L), 1)
    is_fwd_row = lax.broadcasted_iota(jnp.int32, (2 * HEADS, L), 0) < HEADS
    ones = jnp.ones((L, HEAD_DIM), BF16)
    neg_inf = -jnp.inf

    gt_f = jnp.where(is_ctx, ctx_f[3][...], lat_f[3][...])
    gt_b = jnp.where(is_ctx, ctx_b[3][...], lat_b[3][...])
    irows = jnp.where(is_fwd_row, gt_f[:2 * HEADS], gt_b[:2 * HEADS])
    lf = _log_sigmoid(jnp.where(is_fwd_row, gt_f[2 * HEADS:], gt_b[2 * HEADS:]))
    csum, sft = lf, 1
    while sft < L:
        csum = csum + jnp.where(lane >= sft, pltpu.roll(csum, sft, 1), 0.0)
        sft *= 2
    total = csum[:, L - 1:L]
    brow = jnp.where(is_fwd_row, csum, total - csum + lf)
    rowb = irows - brow
    rowb_max = jnp.max(rowb, axis=1, keepdims=True)

    for d, (lat, ctx, h_ref) in enumerate(((lat_f, ctx_f, hf_ref), (lat_b, ctx_b, hb_ref))):
        q, k, v = [jnp.where(is_ctx, c[...], l[...]) for l, c in zip(lat[:3], ctx[:3])]
        mask = (spos <= tpos) if d == 0 else (spos >= tpos)
        for h in range(HEADS):
            idx = d * HEADS + h
            sl = slice(h * HEAD_DIM, (h + 1) * HEAD_DIM)
            qh, kh, vh = q[:, sl], k[:, sl], v[:, sl]
            lf_h, rowb_h = lf[idx:idx + 1, :], rowb[idx:idx + 1, :]
            g_tot = total[idx:idx + 1, :]
            m_in = m_scr[idx][:, 0:1]
            cn = cn_scr[idx]
            b_col = jnp.sum(jnp.where(mask, lf_h, 0.0), axis=1, keepdims=True)
            cmax = jnp.max(jnp.where(mask, rowb_h, neg_inf), axis=1, keepdims=True)
            mm = jnp.maximum(cmax, m_in)
            dmat = jnp.exp(jnp.where(mask, rowb_h - mm, neg_inf))
            s = (lax.dot_general(qh, kh, _NT, preferred_element_type=F32) * dmat).astype(BF16)
            vext = jnp.concatenate([vh, ones], axis=1)
            intra = _dot(s, vext)
            inter = _dot(qh, cn.astype(BF16))
            res = jnp.exp(m_in - mm) * inter + intra
            num, den = res[:, :HEAD_DIM], res[:, HEAD_DIM:]
            floor = jnp.exp(-(b_col + mm))
            h_ref[:, sl] = num / jnp.maximum(jnp.abs(den), floor)
            m_loc = g_tot + rowb_max[idx:idx + 1, :]
            wrow = jnp.exp(rowb_h - rowb_max[idx:idx + 1, :])
            wkt = (kh.astype(F32).T * wrow).astype(BF16)
            ckv = _dot(wkt, vext)
            m_new = jnp.maximum(g_tot + m_in, m_loc)
            a_old = jnp.exp(g_tot + m_in - m_new)
            a_new = jnp.exp(m_loc - m_new)
            cn_scr[idx] = a_old * cn + a_new * ckv
            m_scr[idx] = jnp.broadcast_to(m_new, (1, HEAD_DIM))


def _mlstm(q_l, k_l, v_l, gt_l, q_c, k_c, v_c, gt_c):
    bsz, n, w = q_l.shape
    n_lat = n // CHUNK
    n_ctx = q_c.shape[1] // CHUNK
    steps = n_ctx + n_lat

    def specs(pos_fn):
        tok = pl.BlockSpec((None, CHUNK, w), lambda b, i: (b, pos_fn(i), 0))
        gts = pl.BlockSpec((None, GATE_COLS, CHUNK), lambda b, i: (b, 0, pos_fn(i)))
        return [tok, tok, tok, gts]

    lat_f = lambda i: jnp.maximum(i - n_ctx, 0)
    ctx_f = lambda i: jnp.minimum(i, n_ctx - 1)
    lat_b = lambda i: jnp.minimum(steps - 1 - i, n_lat - 1)
    ctx_b = lambda i: jnp.maximum(n_ctx - 1 - i, 0)
    out = jax.ShapeDtypeStruct((bsz, n, w), F32)
    return pl.pallas_call(
        functools.partial(_mlstm_kernel, n_ctx),
        grid=(bsz, steps),
        in_specs=specs(lat_f) + specs(ctx_f) + specs(lat_b) + specs(ctx_b),
        out_specs=[pl.BlockSpec((None, CHUNK, w), lambda b, i: (b, lat_f(i), 0)),
                   pl.BlockSpec((None, CHUNK, w), lambda b, i: (b, lat_b(i), 0))],
        out_shape=[out, out],
        scratch_shapes=[pltpu.VMEM((2 * HEADS, HEAD_DIM, 2 * HEAD_DIM), F32),
                        pltpu.VMEM((2 * HEADS, 1, HEAD_DIM), F32)],
        compiler_params=_params(("parallel", "arbitrary")),
        name="mlstm",
    )(q_l, k_l, v_l, gt_l, q_c, k_c, v_c, gt_c, q_l, k_l, v_l, gt_l, q_c, k_c, v_c, gt_c)


def _mix_ffn_kernel(x1_ref, hf_ref, hb_ref, o_ref, yp_ref, mod_ref, npre_ref, npost_ref, hn_ref,
                    wout_ref, w1a_ref, w1b_ref, w2_ref, out_ref, mix_scr, y_scr, acc_scr):
    hsum = hf_ref[...] + hb_ref[...]
    for h in range(HEADS):
        sl = slice(h * HEAD_DIM, (h + 1) * HEAD_DIM)
        hh = hsum[:, sl]
        hn = hh * lax.rsqrt(jnp.mean(hh * hh, axis=-1, keepdims=True) + EPS) * hn_ref[:, sl]
        mix_scr[:, sl] = (hn * _sigmoid(o_ref[:, sl])).astype(BF16)
    mix_scr[:, M_W:] = yp_ref[...]
    z = _dot(mix_scr[...], wout_ref[...])
    x2 = x1_ref[...] + (1.0 * mod_ref[0, 5:6, :]) * _rms(z, npost_ref[1:2, :])
    y = _rms(x2, npre_ref[2:3, :]) * (1.0 + mod_ref[0, 7:8, :]) + mod_ref[0, 6:7, :]
    y_scr[...] = y.astype(BF16)
    _swiglu_into(acc_scr, y_scr, w1a_ref, w1b_ref, w2_ref)
    out_ref[...] = x2 + (0.5 * mod_ref[0, 8:9, :]) * _rms(acc_scr[...], npost_ref[2:3, :])


def _mix_ffn(x1, hf, hb, o, yp, mod_all, npre, npost, hnorm, wout, w1a, w1b, w2, *, tm):
    bsz, n, d = x1.shape
    tok = lambda w: pl.BlockSpec((None, tm, w), lambda b, i: (b, i, 0))
    in_specs = [tok(d), tok(M_W), tok(M_W), tok(M_W), tok(POOL_W),
                pl.BlockSpec((1, N_MOD, d), lambda b, i: (b, 0, 0)),
                _const_spec(npre.shape), _const_spec(npost.shape), _const_spec(hnorm.shape),
                _const_spec(wout.shape), _const_spec(w1a.shape), _const_spec(w1b.shape),
                _const_spec(w2.shape)]
    return pl.pallas_call(
        _mix_ffn_kernel,
        grid=(bsz, n // tm),
        in_specs=in_specs,
        out_specs=tok(d),
        out_shape=jax.ShapeDtypeStruct((bsz, n, d), F32),
        scratch_shapes=[pltpu.VMEM((tm, d), BF16), pltpu.VMEM((tm, d), BF16),
                        pltpu.VMEM((tm, d), F32)],
        compiler_params=_params(("parallel", "arbitrary")),
        name="mix_ffn",
    )(x1, hf, hb, o, yp, mod_all, npre, npost, hnorm, wout, w1a, w1b, w2)


def _ffn_weights(w_in, w_out):
    d, two_ff = w_in.shape
    ff = two_ff // 2
    nch = ff // FF_CHUNK
    split = lambda w: w.reshape(d, nch, FF_CHUNK).transpose(1, 0, 2).astype(BF16)
    return split(w_in[:, :ff]), split(w_in[:, ff:]), w_out.reshape(nch, FF_CHUNK, d).astype(BF16)


def kernel(x, c, ctx, c_ctx, w_mod, b_mod, norm_pre, norm_post, ffn_w_in, ffn_w_out, w_in, w_out,
           conv_w, conv_b, w_q, w_k, i_bias, f_bias, head_norm, pool_w, pool_scale):
    assert w_mod.shape[0] == 1, "single-layer stack"
    bsz, n, d = x.shape
    n_ctx = ctx.shape[1]
    m_cols = 3 * M_W + GATE_COLS

    rows = 8
    s_rows = jnp.zeros((rows, d), F32).at[:bsz].set(c).at[bsz].set(c_ctx)
    mod_all = _modulation(s_rows, w_mod[0], b_mod[0]).reshape(rows, N_MOD, d)

    w1a, w1b, w2 = _ffn_weights(ffn_w_in[0, 0], ffn_w_out[0, 0])
    v1a, v1b, v2 = _ffn_weights(ffn_w_in[0, 1], ffn_w_out[0, 1])
    wi = w_in[0]
    gate_w = jnp.pad(wi[:, 3 * M_W:m_cols], ((0, 0), (0, GATE_PAD - GATE_COLS)))
    win_ctx = jnp.concatenate([wi[:, :2 * M_W], gate_w], axis=1).astype(BF16)
    win_lat = jnp.concatenate([win_ctx, wi[:, 2 * M_W:3 * M_W].astype(BF16),
                               wi[:, m_cols:].astype(BF16)], axis=1)
    gbias = jnp.pad(jnp.concatenate([i_bias[0].reshape(-1), f_bias[0].reshape(-1)]),
                    (0, GATE_PAD - GATE_COLS)).reshape(1, GATE_PAD)
    npre, npost = norm_pre[0], norm_post[0]

    tm = min(512, n)
    x1, qk_l, v_l, g_l, o_l, pool_l = _ffn_in(x, mod_all, None, npre, npost, w1a, w1b, w2, win_lat,
                                              gbias, with_rest=True, tm=tm)
    qk_c, v_c, g_c = _ffn_in(ctx, mod_all, bsz, npre, npost, w1a, w1b, w2, win_ctx, gbias,
                             with_rest=False, tm=min(512, n_ctx))

    cb = conv_b[0].reshape(1, M_W)
    wq, wk = w_q[0].astype(BF16), w_k[0].astype(BF16)
    q_l, k_l = _qk_proj(qk_l, conv_w[0], cb, wq, wk, tq=min(1024, n))
    q_c, k_c = _qk_proj(qk_c, conv_w[0], cb, wq, wk, tq=min(1024, n_ctx))
    gt_l = jnp.swapaxes(g_l[:, :, :GATE_COLS], 1, 2)
    gt_c = jnp.swapaxes(g_c[:, :, :GATE_COLS], 1, 2)

    y_pool = _pool_mixer(pool_l, pool_w[0].astype(BF16), pool_scale[0].reshape(1, POOL_W),
                         tt=min(2048, n))
    h_f, h_b = _mlstm(q_l, k_l, v_l, gt_l, q_c, k_c, v_c, gt_c)
    return _mix_ffn(x1, h_f, h_b, o_l, y_pool, mod_all, npre, npost, head_norm[0].reshape(1, M_W),
                    w_out[0].astype(BF16), v1a, v1b, v2, tm=tm)
```

```python
import functools

import jax
import jax.numpy as jnp
from jax import lax
from jax.experimental import pallas as pl
from jax.experimental.pallas import tpu as pltpu

F32 = jnp.float32
BF16 = jnp.bfloat16

EPS = 1e-6
N_MOD = 9
HEADS = 4
HEAD_DIM = 128
M_W = HEADS * HEAD_DIM
POOL_WINDOWS = (2, 4, 8, 16)
POOL_GW = 128
POOL_W = POOL_GW * len(POOL_WINDOWS)
GRID_W = 64
CHUNK = 256
GATE_COLS = 4 * HEADS
GATE_PAD = 128
FF_CHUNK = 256
STATE_ROWS = HEAD_DIM + 16
V7X_VMEM_LIMIT = 56 * 1024 * 1024

_NT = (((1,), (1,)), ((), ()))
_TN = (((0,), (0,)), ((), ()))


def _sigmoid(x):
    return 1.0 / (1.0 + jnp.exp(-x))


def _log_sigmoid(x):
    return jnp.minimum(x, 0.0) - jnp.log(1.0 + jnp.exp(-jnp.abs(x)))


def _rms(x, g):
    ms = jnp.mean(x * x, axis=-1, keepdims=True)
    return x * lax.rsqrt(ms + EPS) * g


def _dot(a, b, dims=None):
    if dims is None:
        return jnp.dot(a, b, preferred_element_type=F32)
    return lax.dot_general(a, b, dims, preferred_element_type=F32)


def _const_spec(shape):
    nd = len(shape)
    return pl.BlockSpec(shape, lambda *_: (0,) * nd, pipeline_mode=pl.Buffered(1))


def _params(sem):
    return pltpu.CompilerParams(dimension_semantics=sem, vmem_limit_bytes=V7X_VMEM_LIMIT)


def _mod_kernel(s_ref, w_ref, b_ref, o_ref):
    s = s_ref[...]
    act = (s * _sigmoid(s)).astype(BF16)
    o_ref[...] = _dot(act, w_ref[...].astype(BF16)) + b_ref[...]


def _modulation(s_rows, w_mod, b_mod):
    rows, d = s_rows.shape
    n = w_mod.shape[1]
    tn = 1024
    return pl.pallas_call(
        _mod_kernel,
        grid=(n // tn,),
        in_specs=[pl.BlockSpec((rows, d), lambda j: (0, 0)),
                  pl.BlockSpec((d, tn), lambda j: (0, j)),
                  pl.BlockSpec((1, tn), lambda j: (0, j))],
        out_specs=pl.BlockSpec((rows, tn), lambda j: (0, j)),
        out_shape=jax.ShapeDtypeStruct((rows, n), F32),
        compiler_params=_params(("arbitrary",)),
        name="mod",
    )(s_rows, w_mod, b_mod.reshape(1, n))


def _swiglu_into(acc_ref, y_ref, w1a_ref, w1b_ref, w2_ref):
    y = y_ref[...]
    for j in range(w1a_ref.shape[0]):
        a = _dot(y, w1a_ref[j])
        b = _dot(y, w1b_ref[j])
        h = ((a * _sigmoid(a)) * b).astype(BF16)
        part = _dot(h, w2_ref[j])
        if j == 0:
            acc_ref[...] = part
        else:
            acc_ref[...] += part


def _ffn_in_kernel(with_rest, x_ref, mod_ref, npre_ref, npost_ref, w1a_ref, w1b_ref, w2_ref,
                   win_ref, wint_ref, gbias_ref, *rest):
    if with_rest:
        x1_ref, qk_ref, g_ref, vt_ref, pool_ref, ot_ref, y_scr, acc_scr = rest
    else:
        qk_ref, g_ref, vt_ref, y_scr, acc_scr = rest
    x = x_ref[...]
    y = _rms(x, npre_ref[0:1, :]) * (1.0 + mod_ref[0, 1:2, :]) + mod_ref[0, 0:1, :]
    y_scr[...] = y.astype(BF16)
    _swiglu_into(acc_scr, y_scr, w1a_ref, w1b_ref, w2_ref)
    x1 = x + (0.5 * mod_ref[0, 2:3, :]) * _rms(acc_scr[...], npost_ref[0:1, :])
    y2 = _rms(x1, npre_ref[1:2, :]) * (1.0 + mod_ref[0, 4:5, :]) + mod_ref[0, 3:4, :]
    y_scr[...] = y2.astype(BF16)
    y2b = y_scr[...]
    c0, c1 = M_W, M_W + GATE_PAD
    qk_ref[...] = _dot(y2b, win_ref[:, 0:c0])
    g_ref[...] = _dot(y2b, win_ref[:, c0:c1]) + gbias_ref[...]
    vt_ref[...] = _dot(wint_ref[0:M_W, :], y2b, _NT).astype(BF16)
    if with_rest:
        x1_ref[...] = x1
        pool_ref[...] = _dot(y2b, win_ref[:, c1:c1 + POOL_W])
        ot_ref[...] = _dot(wint_ref[M_W:2 * M_W, :], y2b, _NT)


def _ffn_in(x, mod_all, mod_row0, npre, npost, w1a, w1b, w2, win, wint, gbias, *, with_rest, tm):
    bsz, n, d = x.shape
    grid = (bsz, n // tm)
    tok = lambda w: pl.BlockSpec((None, tm, w), lambda b, i: (b, i, 0))
    feat = pl.BlockSpec((None, M_W, tm), lambda b, i: (b, 0, i))
    if mod_row0 is None:
        mod_spec = pl.BlockSpec((1, N_MOD, d), lambda b, i: (b, 0, 0))
    else:
        mod_spec = pl.BlockSpec((1, N_MOD, d), lambda b, i: (mod_row0, 0, 0))
    in_specs = [tok(d), mod_spec, _const_spec(npre.shape), _const_spec(npost.shape),
                _const_spec(w1a.shape), _const_spec(w1b.shape), _const_spec(w2.shape),
                _const_spec(win.shape), _const_spec(wint.shape), _const_spec(gbias.shape)]
    sds = lambda w, dt: jax.ShapeDtypeStruct((bsz, n, w), dt)
    fsds = lambda dt: jax.ShapeDtypeStruct((bsz, M_W, n), dt)
    out_specs = [tok(M_W), tok(GATE_PAD), feat]
    out_shape = [sds(M_W, F32), sds(GATE_PAD, F32), fsds(BF16)]
    if with_rest:
        out_specs = [tok(d)] + out_specs + [tok(POOL_W), feat]
        out_shape = [sds(d, F32)] + out_shape + [sds(POOL_W, F32), fsds(F32)]
    return pl.pallas_call(
        functools.partial(_ffn_in_kernel, with_rest),
        grid=grid,
        in_specs=in_specs,
        out_specs=out_specs,
        out_shape=out_shape,
        scratch_shapes=[pltpu.VMEM((tm, d), BF16), pltpu.VMEM((tm, d), F32)],
        compiler_params=_params(("parallel", "arbitrary")),
        name="ffn_in" if with_rest else "ffn_in_ctx",
    )(x, mod_all, npre, npost, w1a, w1b, w2, win, wint, gbias)


def _qk_kernel(nt, x_ref, prev_ref, next_ref, cw_ref, cb_ref, wqt_ref, wkt_ref, qt_ref, kt_ref):
    j = pl.program_id(1)
    x = x_ref[...]
    tq = x.shape[0]
    prev_row = prev_ref[7:8, :] * jnp.where(j > 0, 1.0, 0.0)
    next_row = next_ref[0:1, :] * jnp.where(j < nt - 1, 1.0, 0.0)
    row = lax.broadcasted_iota(jnp.int32, x.shape, 0)
    up = jnp.where(row == 0, prev_row, pltpu.roll(x, 1, 0))
    dn = jnp.where(row == tq - 1, next_row, pltpu.roll(x, tq - 1, 0))
    conv = cb_ref[...] + up * cw_ref[0:1, :] + x * cw_ref[1:2, :] + dn * cw_ref[2:3, :]
    u = (conv * _sigmoid(conv)).astype(BF16)
    scale = HEAD_DIM ** -0.5
    for h in range(HEADS):
        sl = slice(h * HEAD_DIM, (h + 1) * HEAD_DIM)
        uh = u[:, sl]
        qt_ref[sl, :] = _dot(wqt_ref[h], uh, _NT).astype(BF16)
        kt_ref[sl, :] = (_dot(wkt_ref[h], uh, _NT) * scale).astype(BF16)


def _qk_proj(qk_src, conv_w, conv_b, wqt, wkt, *, tq):
    bsz, n, w = qk_src.shape
    nt = n // tq
    blk8 = tq // 8
    last8 = n // 8 - 1
    tok = pl.BlockSpec((None, tq, w), lambda b, j: (b, j, 0))
    prev = pl.BlockSpec((None, 8, w), lambda b, j: (b, jnp.maximum(j * blk8 - 1, 0), 0))
    nxt = pl.BlockSpec((None, 8, w), lambda b, j: (b, jnp.minimum((j + 1) * blk8, last8), 0))
    feat = pl.BlockSpec((None, w, tq), lambda b, j: (b, 0, j))
    out = jax.ShapeDtypeStruct((bsz, w, n), BF16)
    return pl.pallas_call(
        functools.partial(_qk_kernel, nt),
        grid=(bsz, nt),
        in_specs=[tok, prev, nxt, _const_spec(conv_w.shape), _const_spec(conv_b.shape),
                  _const_spec(wqt.shape), _const_spec(wkt.shape)],
        out_specs=[feat, feat],
        out_shape=[out, out],
        compiler_params=_params(("parallel", "arbitrary")),
        name="qk",
    )(qk_src, qk_src, qk_src, conv_w, conv_b, wqt, wkt)


def _pool_kernel(nt, grid_h, x_ref, prev_ref, next_ref, wp_ref, ps_ref, y_ref, xs_ref):
    j = pl.program_id(1)
    tt = x_ref.shape[0]
    halo = prev_ref.shape[0]
    rows_per_tile = tt // GRID_W
    xs_ref[0:halo, :] = prev_ref[...] * jnp.where(j > 0, 1.0, 0.0)
    xs_ref[halo:halo + tt, :] = x_ref[...]
    xs_ref[halo + tt:halo + tt + halo, :] = next_ref[...] * jnp.where(j < nt - 1, 1.0, 0.0)
    t = lax.broadcasted_iota(jnp.int32, (tt, POOL_GW), 0)
    c = jnp.bitwise_and(t, GRID_W - 1)
    r = j * rows_per_tile + jnp.right_shift(t, 6)

    def from_left(v, k):
        return jnp.where(c >= k, pltpu.roll(v, k, 0), 0.0)

    def from_right(v, k):
        return jnp.where(c < GRID_W - k, pltpu.roll(v, tt - k, 0), 0.0)

    for g, win in enumerate(POOL_WINDOWS):
        lo, hi = win // 2, win - 1 - win // 2
        sl = slice(g * POOL_GW, (g + 1) * POOL_GW)
        vs = None
        for dr in range(-lo, hi + 1):
            start = halo + dr * GRID_W
            piece = xs_ref[start:start + tt, sl]
            vs = piece if vs is None else vs + piece
        trail, lead, k = vs, vs, 1
        while k < lo:
            trail = trail + from_left(trail, k)
            lead = lead + from_right(lead, k)
            k *= 2
        tot = from_left(trail, 1) + lead
        cnt_c = jnp.minimum(c + hi, GRID_W - 1) - jnp.maximum(c - lo, 0) + 1
        cnt_r = jnp.minimum(r + hi, grid_h - 1) - jnp.maximum(r - lo, 0) + 1
        pg = tot / (cnt_c * cnt_r).astype(F32)
        dlt = (pg - x_ref[:, sl]).astype(BF16)
        y_ref[:, sl] = (_dot(dlt, wp_ref[g]) * ps_ref[:, sl]).astype(BF16)


def _pool_mixer(u, wp, ps, *, tt):
    bsz, n, w = u.shape
    nt = n // tt
    halo = 8 * GRID_W
    hb = tt // halo
    last = n // halo - 1
    tok = pl.BlockSpec((None, tt, w), lambda b, j: (b, j, 0))
    prev = pl.BlockSpec((None, halo, w), lambda b, j: (b, jnp.maximum(j * hb - 1, 0), 0))
    nxt = pl.BlockSpec((None, halo, w), lambda b, j: (b, jnp.minimum((j + 1) * hb, last), 0))
    return pl.pallas_call(
        functools.partial(_pool_kernel, nt, n // GRID_W),
        grid=(bsz, nt),
        in_specs=[tok, prev, nxt, _const_spec(wp.shape), _const_spec(ps.shape)],
        out_specs=tok,
        out_shape=jax.ShapeDtypeStruct((bsz, n, w), BF16),
        scratch_shapes=[pltpu.VMEM((tt + 2 * halo, w), F32)],
        compiler_params=_params(("parallel", "arbitrary")),
        name="pool",
    )(u, u, u, wp, ps)


def _split3(x):
    hi = x.astype(BF16).astype(F32)
    r = x - hi
    mid = r.astype(BF16).astype(F32)
    lo = (r - mid).astype(BF16).astype(F32)
    return hi, mid, lo


def _gate_scan_kernel(chunk, gt_ref, o_ref):
    n_combo = 2 * HEADS
    g = gt_ref[...]
    n = g.shape[1]
    pos = jnp.bitwise_and(lax.broadcasted_iota(jnp.int32, (n_combo, n), 1), chunk - 1)
    is_fwd_row = lax.broadcasted_iota(jnp.int32, (n_combo, n), 0) < HEADS

    def scan(x, op, ident):
        sft = 1
        while sft < chunk:
            from_l = jnp.where(pos >= sft, pltpu.roll(x, sft, 1), ident)
            from_r = jnp.where(pos < chunk - sft, pltpu.roll(x, n - sft, 1), ident)
            x = op(x, jnp.where(is_fwd_row, from_l, from_r))
            sft *= 2
        return x

    brow = scan(_log_sigmoid(g[n_combo:]), jnp.add, 0.0)
    rowb = g[:n_combo] - brow
    o_ref[0:n_combo, :] = rowb
    o_ref[n_combo:2 * n_combo, :] = brow
    o_ref[2 * n_combo:, :] = scan(rowb, jnp.maximum, -jnp.inf)


def _gate_scan(gt):
    bsz, rows, n = gt.shape
    return pl.pallas_call(
        functools.partial(_gate_scan_kernel, CHUNK),
        grid=(bsz,),
        in_specs=[pl.BlockSpec((None, rows, n), lambda b: (b, 0, 0))],
        out_specs=pl.BlockSpec((None, 3 * rows // 2, n), lambda b: (b, 0, 0)),
        out_shape=jax.ShapeDtypeStruct((bsz, 3 * rows // 2, n), F32),
        compiler_params=_params(("parallel",)),
        name="gate_scan",
    )(gt)


def _mlstm_kernel(emit_h, emit_state, *refs):
    fwd, bwd = refs[0:4], refs[4:8]
    ct0_ref, m0_ref = refs[8:10]
    outs = list(refs[10:])
    hf_ref, hb_ref = (outs.pop(0), outs.pop(0)) if emit_h else (None, None)
    cto_ref, mo_ref = (outs.pop(0), outs.pop(0)) if emit_state else (None, None)
    ct_scr, m_scr = outs
    i = pl.program_id(0)
    n_combo = 2 * HEADS
    bsz, _, L = fwd[0].shape

    @pl.when(i == 0)
    def _():
        ct_scr[...] = ct0_ref[...]
        m_scr[...] = m0_ref[...]

    is_fwd_row = lax.broadcasted_iota(jnp.int32, (n_combo, L), 0) < HEADS
    is_fwd_col = lax.broadcasted_iota(jnp.int32, (n_combo, 1), 0) < HEADS
    krow = lax.broadcasted_iota(jnp.int32, (8, L), 0)
    tpos = lax.broadcasted_iota(jnp.int32, (L, L), 0)
    spos = lax.broadcasted_iota(jnp.int32, (L, L), 1)
    ones_rows = jnp.ones((STATE_ROWS - HEAD_DIM, L), BF16)

    for b in range(bsz):
        sc_f, sc_b = fwd[3][b], bwd[3][b]
        pick = lambda r: jnp.where(is_fwd_row, sc_f[r * n_combo:(r + 1) * n_combo],
                                   sc_b[r * n_combo:(r + 1) * n_combo])
        last = lambda a: jnp.where(is_fwd_col, a[:, L - 1:L], a[:, 0:1])
        rowb, brow, cmax = pick(0), pick(1), pick(2)
        total, rowb_max = last(brow), last(cmax)
        m_in = m_scr[b][:, 0:1]
        mm = jnp.maximum(cmax, m_in)
        w_inter = jnp.exp(m_in - mm)
        floor = jnp.exp(-(brow + mm))
        w_state = jnp.exp(rowb - rowb_max)
        m_loc = total + rowb_max
        m_new = jnp.maximum(total + m_in, m_loc)
        a_old = jnp.exp(total + m_in - m_new)
        a_new = jnp.exp(m_loc - m_new)
        m_scr[b] = jnp.broadcast_to(m_new, (n_combo, HEAD_DIM))
        rb3, nm3 = _split3(rowb), _split3(-mm)

        for d, (src, h_ref) in enumerate(((fwd, hf_ref), (bwd, hb_ref))):
            mask = (spos <= tpos) if d == 0 else (spos >= tpos)
            for h in range(HEADS):
                c = d * HEADS + h
                sl = slice(h * HEAD_DIM, (h + 1) * HEAD_DIM)
                row = lambda a: a[c:c + 1, :]
                qt, kt, vt = src[0][b, sl, :], src[1][b, sl, :], src[2][b, sl, :]
                lhs = jnp.where(krow < 3, 1.0, jnp.where(krow == 3, row(nm3[0]), jnp.where(
                    krow == 4, row(nm3[1]), jnp.where(krow == 5, row(nm3[2]), 0.0))))
                rhs = jnp.where(krow == 0, row(rb3[0]), jnp.where(krow == 1, row(rb3[1]), jnp.where(
                    krow == 2, row(rb3[2]), jnp.where(krow < 6, 1.0, 0.0))))
                expo = _dot(lhs.astype(BF16), rhs.astype(BF16), _TN)
                dmat = jnp.where(mask, jnp.exp(expo), 0.0)
                s = (_dot(qt, kt, _TN) * dmat).astype(BF16)
                vext = jnp.concatenate([vt, ones_rows], axis=0)
                ct = ct_scr[b * n_combo + c]
                state_t = _dot(ct.astype(BF16), qt)
                if emit_h:
                    res = row(w_inter) * state_t + _dot(vext, s, _NT)
                    den = res[HEAD_DIM:HEAD_DIM + 1, :]
                    h_ref[b, sl, :] = res[:HEAD_DIM] / jnp.maximum(jnp.abs(den), row(floor))
                ktw = (kt.astype(F32) * row(w_state)).astype(BF16)
                ckv = _dot(vext, ktw, _NT)
                ct_scr[b * n_combo + c] = a_old[c:c + 1, :] * ct + a_new[c:c + 1, :] * ckv

    if emit_state:
        @pl.when(i == pl.num_programs(0) - 1)
        def _():
            cto_ref[...] = ct_scr[...]
            mo_ref[...] = m_scr[...]


def _mlstm(qt, kt, vt, sc, ct0, m0, *, emit_h):
    bsz, w, n = qt.shape
    nc = n // CHUNK

    def specs(pos_fn):
        feat = pl.BlockSpec((bsz, w, CHUNK), lambda i: (0, 0, pos_fn(i)))
        scs = pl.BlockSpec((bsz, sc.shape[1], CHUNK), lambda i: (0, 0, pos_fn(i)))
        return [feat, feat, feat, scs]

    fpos = lambda i: i
    bpos = lambda i: nc - 1 - i
    ct_spec = pl.BlockSpec(ct0.shape, lambda i: (0, 0, 0))
    m_spec = pl.BlockSpec(m0.shape, lambda i: (0, 0, 0))
    if emit_h:
        hsd = jax.ShapeDtypeStruct((bsz, w, n), F32)
        out_specs = [pl.BlockSpec((bsz, w, CHUNK), lambda i: (0, 0, fpos(i))),
                     pl.BlockSpec((bsz, w, CHUNK), lambda i: (0, 0, bpos(i)))]
        out_shape = [hsd, hsd]
    else:
        out_specs = [ct_spec, m_spec]
        out_shape = [jax.ShapeDtypeStruct(ct0.shape, F32), jax.ShapeDtypeStruct(m0.shape, F32)]
    return pl.pallas_call(
        functools.partial(_mlstm_kernel, emit_h, not emit_h),
        grid=(nc,),
        in_specs=specs(fpos) + specs(bpos) + [ct_spec, m_spec],
        out_specs=out_specs,
        out_shape=out_shape,
        scratch_shapes=[pltpu.VMEM(ct0.shape, F32), pltpu.VMEM(m0.shape, F32)],
        compiler_params=_params(("arbitrary",)),
        name="mlstm" if emit_h else "mlstm_ctx",
    )(qt, kt, vt, sc, qt, kt, vt, sc, ct0, m0)


def _mix_ffn_kernel(x1_ref, hf_ref, hb_ref, ot_ref, yp_ref, mod_ref, npre_ref, npost_ref, hn_ref,
                    wout_ref, w1a_ref, w1b_ref, w2_ref, out_ref, mixt_scr, y_scr, acc_scr):
    tm = x1_ref.shape[0]
    for h in range(HEADS):
        sl = slice(h * HEAD_DIM, (h + 1) * HEAD_DIM)
        hh = hf_ref[sl, :] + hb_ref[sl, :]
        inv = lax.rsqrt(jnp.mean(hh * hh, axis=0, keepdims=True) + EPS)
        hn = hh * inv * pltpu.repeat(hn_ref[sl, :], tm // HEAD_DIM, axis=1)
        mixt_scr[sl, :] = (hn * _sigmoid(ot_ref[sl, :])).astype(BF16)
    z = _dot(mixt_scr[...], wout_ref[0:M_W, :], _TN) + _dot(yp_ref[...], wout_ref[M_W:, :])
    x2 = x1_ref[...] + (1.0 * mod_ref[0, 5:6, :]) * _rms(z, npost_ref[1:2, :])
    y = _rms(x2, npre_ref[2:3, :]) * (1.0 + mod_ref[0, 7:8, :]) + mod_ref[0, 6:7, :]
    y_scr[...] = y.astype(BF16)
    _swiglu_into(acc_scr, y_scr, w1a_ref, w1b_ref, w2_ref)
    out_ref[...] = x2 + (0.5 * mod_ref[0, 8:9, :]) * _rms(acc_scr[...], npost_ref[2:3, :])


def _mix_ffn(x1, hf, hb, ot, yp, mod_all, npre, npost, hnorm, wout, w1a, w1b, w2, *, tm):
    bsz, n, d = x1.shape
    tok = lambda w: pl.BlockSpec((None, tm, w), lambda b, i: (b, i, 0))
    feat = pl.BlockSpec((None, M_W, tm), lambda b, i: (b, 0, i))
    in_specs = [tok(d), feat, feat, feat, tok(POOL_W),
                pl.BlockSpec((1, N_MOD, d), lambda b, i: (b, 0, 0)),
                _const_spec(npre.shape), _const_spec(npost.shape), _const_spec(hnorm.shape),
                _const_spec(wout.shape), _const_spec(w1a.shape), _const_spec(w1b.shape),
                _const_spec(w2.shape)]
    return pl.pallas_call(
        _mix_ffn_kernel,
        grid=(bsz, n // tm),
        in_specs=in_specs,
        out_specs=tok(d),
        out_shape=jax.ShapeDtypeStruct((bsz, n, d), F32),
        scratch_shapes=[pltpu.VMEM((M_W, tm), BF16), pltpu.VMEM((tm, d), BF16),
                        pltpu.VMEM((tm, d), F32)],
        compiler_params=_params(("parallel", "arbitrary")),
        name="mix_ffn",
    )(x1, hf, hb, ot, yp, mod_all, npre, npost, hnorm, wout, w1a, w1b, w2)


def _ffn_weights(w_in, w_out):
    d, two_ff = w_in.shape
    ff = two_ff // 2
    nch = ff // FF_CHUNK
    split = lambda w: w.reshape(d, nch, FF_CHUNK).transpose(1, 0, 2).astype(BF16)
    return split(w_in[:, :ff]), split(w_in[:, ff:]), w_out.reshape(nch, FF_CHUNK, d).astype(BF16)


def kernel(x, c, ctx, c_ctx, w_mod, b_mod, norm_pre, norm_post, ffn_w_in, ffn_w_out, w_in, w_out,
           conv_w, conv_b, w_q, w_k, i_bias, f_bias, head_norm, pool_w, pool_scale):
    assert w_mod.shape[0] == 1, "single-layer stack"
    bsz, n, d = x.shape
    n_ctx = ctx.shape[1]
    m_cols = 3 * M_W + GATE_COLS

    rows = 8
    s_rows = jnp.zeros((rows, d), F32).at[:bsz].set(c).at[bsz].set(c_ctx)
    mod_all = _modulation(s_rows, w_mod[0], b_mod[0]).reshape(rows, N_MOD, d)

    w1a, w1b, w2 = _ffn_weights(ffn_w_in[0, 0], ffn_w_out[0, 0])
    v1a, v1b, v2 = _ffn_weights(ffn_w_in[0, 1], ffn_w_out[0, 1])
    wi = w_in[0]
    gate_w = jnp.pad(wi[:, 3 * M_W:m_cols], ((0, 0), (0, GATE_PAD - GATE_COLS)))
    win_ctx = jnp.concatenate([wi[:, :M_W], gate_w], axis=1).astype(BF16)
    win_lat = jnp.concatenate([win_ctx, wi[:, m_cols:].astype(BF16)], axis=1)
    wint_lat = wi[:, M_W:3 * M_W].T.astype(BF16)
    wint_ctx = wint_lat[:M_W]
    gbias = jnp.pad(jnp.concatenate([i_bias[0].reshape(-1), f_bias[0].reshape(-1)]),
                    (0, GATE_PAD - GATE_COLS)).reshape(1, GATE_PAD)
    npre, npost = norm_pre[0], norm_post[0]

    tm = min(512, n)
    x1, qk_l, g_l, vt_l, pool_l, ot_l = _ffn_in(x, mod_all, None, npre, npost, w1a, w1b, w2,
                                                win_lat, wint_lat, gbias, with_rest=True, tm=tm)
    qk_c, g_c, vt_c = _ffn_in(ctx, mod_all, bsz, npre, npost, w1a, w1b, w2, win_ctx, wint_ctx,
                              gbias, with_rest=False, tm=min(512, n_ctx))

    cb = conv_b[0].reshape(1, M_W)
    wqt = jnp.swapaxes(w_q[0], 1, 2).astype(BF16)
    wkt = jnp.swapaxes(w_k[0], 1, 2).astype(BF16)
    qt_l, kt_l = _qk_proj(qk_l, conv_w[0], cb, wqt, wkt, tq=min(1024, n))
    qt_c, kt_c = _qk_proj(qk_c, conv_w[0], cb, wqt, wkt, tq=min(1024, n_ctx))
    gt_l = jnp.swapaxes(g_l[:, :, :GATE_COLS], 1, 2)
    gt_c = jnp.swapaxes(g_c[:, :, :GATE_COLS], 1, 2)

    y_pool = _pool_mixer(pool_l, pool_w[0].astype(BF16), pool_scale[0].reshape(1, POOL_W),
                         tt=min(2048, n))

    ct0 = jnp.zeros((bsz * 2 * HEADS, STATE_ROWS, HEAD_DIM), F32)
    m0 = jnp.zeros((bsz, 2 * HEADS, HEAD_DIM), F32)
    ct1, m1 = _mlstm(qt_c, kt_c, vt_c, _gate_scan(gt_c), ct0, m0, emit_h=False)
    ht_f, ht_b = _mlstm(qt_l, kt_l, vt_l, _gate_scan(gt_l), ct1, m1, emit_h=True)

    hnorm = jnp.broadcast_to(head_norm[0].reshape(M_W, 1), (M_W, HEAD_DIM))
    return _mix_ffn(x1, ht_f, ht_b, ot_l, y_pool, mod_all, npre, npost, hnorm,
                    w_out[0].astype(BF16), v1a, v1b, v2, tm=tm)
```

```python
import functools

import jax
import jax.numpy as jnp
from jax import lax
from jax.experimental import pallas as pl
from jax.experimental.pallas import tpu as pltpu

F32 = jnp.float32
BF16 = jnp.bfloat16

EPS = 1e-6
N_MOD = 9
HEADS = 4
HEAD_DIM = 128
M_W = HEADS * HEAD_DIM
POOL_WINDOWS = (2, 4, 8, 16)
POOL_GW = 128
POOL_W = POOL_GW * len(POOL_WINDOWS)
GRID_W = 64
POOL_PAD = 8
CHUNK = 256
GATE_COLS = 4 * HEADS
GATE_PAD = 128
FF_CHUNK = 256
STATE_ROWS = HEAD_DIM + 16
V7X_VMEM_LIMIT = 56 * 1024 * 1024

_NT = (((1,), (1,)), ((), ()))
_TN = (((0,), (0,)), ((), ()))


def _sigmoid(x):
    return 1.0 / (1.0 + jnp.exp(-x))


def _log_sigmoid(x):
    return jnp.minimum(x, 0.0) - jnp.log(1.0 + jnp.exp(-jnp.abs(x)))


def _rms(x, g):
    ms = jnp.mean(x * x, axis=-1, keepdims=True)
    return x * lax.rsqrt(ms + EPS) * g


def _dot(a, b, dims=None):
    if dims is None:
        return jnp.dot(a, b, preferred_element_type=F32)
    return lax.dot_general(a, b, dims, preferred_element_type=F32)


def _const_spec(shape):
    nd = len(shape)
    return pl.BlockSpec(shape, lambda *_: (0,) * nd, pipeline_mode=pl.Buffered(1))


def _params(sem):
    return pltpu.CompilerParams(dimension_semantics=sem, vmem_limit_bytes=V7X_VMEM_LIMIT)


def _mod_kernel(s_ref, w_ref, b_ref, o_ref):
    s = s_ref[...]
    act = (s * _sigmoid(s)).astype(BF16)
    o_ref[...] = _dot(act, w_ref[...].astype(BF16)) + b_ref[...]


def _modulation(s_rows, w_mod, b_mod):
    rows, d = s_rows.shape
    n = w_mod.shape[1]
    tn = 1024
    return pl.pallas_call(
        _mod_kernel,
        grid=(n // tn,),
        in_specs=[pl.BlockSpec((rows, d), lambda j: (0, 0)),
                  pl.BlockSpec((d, tn), lambda j: (0, j)),
                  pl.BlockSpec((1, tn), lambda j: (0, j))],
        out_specs=pl.BlockSpec((rows, tn), lambda j: (0, j)),
        out_shape=jax.ShapeDtypeStruct((rows, n), F32),
        compiler_params=_params(("arbitrary",)),
        name="mod",
    )(s_rows, w_mod, b_mod.reshape(1, n))


def _swiglu_into(acc_ref, y_ref, w1a_ref, w1b_ref, w2_ref):
    y = y_ref[...]
    for j in range(w2_ref.shape[0] // FF_CHUNK):
        cols = slice(j * FF_CHUNK, (j + 1) * FF_CHUNK)
        a = _dot(y, w1a_ref[:, cols])
        b = _dot(y, w1b_ref[:, cols])
        h = ((a * _sigmoid(a)) * b).astype(BF16)
        part = _dot(h, w2_ref[cols, :])
        if j == 0:
            acc_ref[...] = part
        else:
            acc_ref[...] += part


def _ffn_in_kernel(with_rest, x_ref, mod_ref, npre_ref, npost_ref, w1a_ref, w1b_ref, w2_ref,
                   win_ref, wint_ref, gbias_ref, *rest):
    if with_rest:
        x1_ref, qk_ref, g_ref, vt_ref, pool_ref, ot_ref, y_scr, acc_scr = rest
    else:
        qk_ref, g_ref, vt_ref, y_scr, acc_scr = rest
    x = x_ref[...]
    y = _rms(x, npre_ref[0:1, :]) * (1.0 + mod_ref[0, 1:2, :]) + mod_ref[0, 0:1, :]
    y_scr[...] = y.astype(BF16)
    _swiglu_into(acc_scr, y_scr, w1a_ref, w1b_ref, w2_ref)
    x1 = x + (0.5 * mod_ref[0, 2:3, :]) * _rms(acc_scr[...], npost_ref[0:1, :])
    y2 = _rms(x1, npre_ref[1:2, :]) * (1.0 + mod_ref[0, 4:5, :]) + mod_ref[0, 3:4, :]
    y_scr[...] = y2.astype(BF16)
    y2b = y_scr[...]
    c0, c1 = M_W, M_W + GATE_PAD
    qk_ref[...] = _dot(y2b, win_ref[:, 0:c0])
    g_ref[...] = _dot(y2b, win_ref[:, c0:c1]) + gbias_ref[...]
    vt_ref[...] = _dot(wint_ref[0:M_W, :], y2b, _NT).astype(BF16)
    if with_rest:
        x1_ref[...] = x1
        pool_ref[...] = _dot(y2b, win_ref[:, c1:c1 + POOL_W])
        ot_ref[...] = _dot(wint_ref[M_W:2 * M_W, :], y2b, _NT)


def _ffn_in(x, mod_all, mod_row0, npre, npost, w1a, w1b, w2, win, wint, gbias, *, with_rest, tm):
    bsz, n, d = x.shape
    grid = (bsz, n // tm)
    tok = lambda w: pl.BlockSpec((None, tm, w), lambda b, i: (b, i, 0))
    feat = pl.BlockSpec((None, M_W, tm), lambda b, i: (b, 0, i))
    if mod_row0 is None:
        mod_spec = pl.BlockSpec((1, N_MOD, d), lambda b, i: (b, 0, 0))
    else:
        mod_spec = pl.BlockSpec((1, N_MOD, d), lambda b, i: (mod_row0, 0, 0))
    in_specs = [tok(d), mod_spec, _const_spec(npre.shape), _const_spec(npost.shape),
                _const_spec(w1a.shape), _const_spec(w1b.shape), _const_spec(w2.shape),
                _const_spec(win.shape), _const_spec(wint.shape), _const_spec(gbias.shape)]
    sds = lambda w, dt: jax.ShapeDtypeStruct((bsz, n, w), dt)
    fsds = lambda dt: jax.ShapeDtypeStruct((bsz, M_W, n), dt)
    out_specs = [tok(M_W), tok(GATE_PAD), feat]
    out_shape = [sds(M_W, F32), sds(GATE_PAD, F32), fsds(BF16)]
    if with_rest:
        out_specs = [tok(d)] + out_specs + [tok(POOL_W), feat]
        out_shape = [sds(d, F32)] + out_shape + [sds(POOL_W, F32), fsds(F32)]
    return pl.pallas_call(
        functools.partial(_ffn_in_kernel, with_rest),
        grid=grid,
        in_specs=in_specs,
        out_specs=out_specs,
        out_shape=out_shape,
        scratch_shapes=[pltpu.VMEM((tm, d), BF16), pltpu.VMEM((tm, d), F32)],
        compiler_params=_params(("parallel", "arbitrary")),
        name="ffn_in" if with_rest else "ffn_in_ctx",
    )(x, mod_all, npre, npost, w1a, w1b, w2, win, wint, gbias)


def _qk_kernel(nt, x_ref, prev_ref, next_ref, cw_ref, cb_ref, wqt_ref, wkt_ref, qt_ref, kt_ref,
               xs_ref):
    j = pl.program_id(1)
    x = x_ref[...]
    tq = x.shape[0]
    xs_ref[0:8, :] = prev_ref[...] * jnp.where(j > 0, 1.0, 0.0)
    xs_ref[8:8 + tq, :] = x
    xs_ref[8 + tq:16 + tq, :] = next_ref[...] * jnp.where(j < nt - 1, 1.0, 0.0)
    up = xs_ref[7:7 + tq, :]
    dn = xs_ref[9:9 + tq, :]
    conv = cb_ref[...] + up * cw_ref[0:1, :] + x * cw_ref[1:2, :] + dn * cw_ref[2:3, :]
    u = (conv * _sigmoid(conv)).astype(BF16)
    scale = HEAD_DIM ** -0.5
    for h in range(HEADS):
        sl = slice(h * HEAD_DIM, (h + 1) * HEAD_DIM)
        uh = u[:, sl]
        qt_ref[sl, :] = _dot(wqt_ref[h], uh, _NT).astype(BF16)
        kt_ref[sl, :] = (_dot(wkt_ref[h], uh, _NT) * scale).astype(BF16)


def _qk_proj(qk_src, conv_w, conv_b, wqt, wkt, *, tq):
    bsz, n, w = qk_src.shape
    nt = n // tq
    blk8 = tq // 8
    last8 = n // 8 - 1
    tok = pl.BlockSpec((None, tq, w), lambda b, j: (b, j, 0))
    prev = pl.BlockSpec((None, 8, w), lambda b, j: (b, jnp.maximum(j * blk8 - 1, 0), 0))
    nxt = pl.BlockSpec((None, 8, w), lambda b, j: (b, jnp.minimum((j + 1) * blk8, last8), 0))
    feat = pl.BlockSpec((None, w, tq), lambda b, j: (b, 0, j))
    out = jax.ShapeDtypeStruct((bsz, w, n), BF16)
    return pl.pallas_call(
        functools.partial(_qk_kernel, nt),
        grid=(bsz, nt),
        in_specs=[tok, prev, nxt, _const_spec(conv_w.shape), _const_spec(conv_b.shape),
                  _const_spec(wqt.shape), _const_spec(wkt.shape)],
        out_specs=[feat, feat],
        out_shape=[out, out],
        scratch_shapes=[pltpu.VMEM((tq + 16, w), F32)],
        compiler_params=_params(("parallel", "arbitrary")),
        name="qk",
    )(qk_src, qk_src, qk_src, conv_w, conv_b, wqt, wkt)


def _pool_kernel(nt, grid_h, x_ref, prev_ref, next_ref, wp_ref, ps_ref, y_ref, xs_ref, *bufs):
    j = pl.program_id(1)
    tt = x_ref.shape[0]
    halo = prev_ref.shape[0]
    halo_rows = halo // GRID_W
    rows = tt // GRID_W
    d0, d1 = POOL_PAD, POOL_PAD + GRID_W
    xs_ref[0:halo, :] = prev_ref[...] * jnp.where(j > 0, 1.0, 0.0)
    xs_ref[halo:halo + tt, :] = x_ref[...]
    xs_ref[halo + tt:halo + tt + halo, :] = next_ref[...] * jnp.where(j < nt - 1, 1.0, 0.0)
    zpad = jnp.zeros((rows, POOL_PAD, POOL_GW), F32)
    for buf in bufs:
        buf[:, 0:d0, :] = zpad
        buf[:, d1:d1 + POOL_PAD, :] = zpad
    c = lax.broadcasted_iota(jnp.int32, (1, GRID_W, POOL_GW), 1)
    r = j * rows + lax.broadcasted_iota(jnp.int32, (rows, 1, POOL_GW), 0)

    def shifted_sum(src, dst, k):
        dst[:, d0:d1, :] = src[:, d0:d1, :] + src[:, d0 + k:d1 + k, :]

    for g, win in enumerate(POOL_WINDOWS):
        lo, hi = win // 2, win - 1 - win // 2
        sl = slice(g * POOL_GW, (g + 1) * POOL_GW)
        first = (halo_rows - lo) * GRID_W
        vs = xs_ref[first:first + (rows + win - 1) * GRID_W, sl]
        span = 1
        while span < win:
            vs = vs[:-span * GRID_W] + vs[span * GRID_W:]
            span *= 2
        a, b, cc, dd = bufs
        a[:, d0:d1, :] = vs.reshape(rows, GRID_W, POOL_GW)
        trail, lead, spare_t, spare_l, k = a, a, b, dd, 1
        while k < lo:
            shifted_sum(trail, spare_t, -k)
            shifted_sum(lead, spare_l, k)
            trail, spare_t = spare_t, (cc if spare_t is b else b)
            lead, spare_l = spare_l, (a if spare_l is dd else dd)
            k *= 2
        tot = trail[:, d0 - 1:d1 - 1, :] + lead[:, d0:d1, :]
        cnt_c = jnp.minimum(c + hi, GRID_W - 1) - jnp.maximum(c - lo, 0) + 1
        cnt_r = jnp.minimum(r + hi, grid_h - 1) - jnp.maximum(r - lo, 0) + 1
        pg = tot * (1.0 / cnt_c.astype(F32)) * (1.0 / cnt_r.astype(F32))
        dlt = (pg.reshape(tt, POOL_GW) - x_ref[:, sl]).astype(BF16)
        y_ref[:, sl] = (_dot(dlt, wp_ref[g]) * ps_ref[:, sl]).astype(BF16)


def _pool_mixer(u, wp, ps, *, tt):
    bsz, n, w = u.shape
    nt = n // tt
    halo = 8 * GRID_W
    hb = tt // halo
    last = n // halo - 1
    tok = pl.BlockSpec((None, tt, w), lambda b, j: (b, j, 0))
    prev = pl.BlockSpec((None, halo, w), lambda b, j: (b, jnp.maximum(j * hb - 1, 0), 0))
    nxt = pl.BlockSpec((None, halo, w), lambda b, j: (b, jnp.minimum((j + 1) * hb, last), 0))
    return pl.pallas_call(
        functools.partial(_pool_kernel, nt, n // GRID_W),
        grid=(bsz, nt),
        in_specs=[tok, prev, nxt, _const_spec(wp.shape), _const_spec(ps.shape)],
        out_specs=tok,
        out_shape=jax.ShapeDtypeStruct((bsz, n, w), BF16),
        scratch_shapes=[pltpu.VMEM((tt + 2 * halo, w), F32)]
        + [pltpu.VMEM((tt // GRID_W, GRID_W + 2 * POOL_PAD, POOL_GW), F32)] * 4,
        compiler_params=_params(("parallel", "arbitrary")),
        name="pool",
    )(u, u, u, wp, ps)


def _split3(x):
    hi = x.astype(BF16).astype(F32)
    r = x - hi
    mid = r.astype(BF16).astype(F32)
    lo = (r - mid).astype(BF16).astype(F32)
    return hi, mid, lo


def _gate_scan_kernel(chunk, gt_ref, o_ref):
    n_combo = 2 * HEADS
    g = gt_ref[...]
    n = g.shape[1]
    pos = jnp.bitwise_and(lax.broadcasted_iota(jnp.int32, (n_combo, n), 1), chunk - 1)
    is_fwd_row = lax.broadcasted_iota(jnp.int32, (n_combo, n), 0) < HEADS

    def scan(x, op, ident):
        sft = 1
        while sft < chunk:
            from_l = jnp.where(pos >= sft, pltpu.roll(x, sft, 1), ident)
            from_r = jnp.where(pos < chunk - sft, pltpu.roll(x, n - sft, 1), ident)
            x = op(x, jnp.where(is_fwd_row, from_l, from_r))
            sft *= 2
        return x

    brow = scan(_log_sigmoid(g[n_combo:]), jnp.add, 0.0)
    rowb = g[:n_combo] - brow
    o_ref[0:n_combo, :] = rowb
    o_ref[n_combo:2 * n_combo, :] = brow
    o_ref[2 * n_combo:, :] = scan(rowb, jnp.maximum, -jnp.inf)


def _gate_scan(gt):
    bsz, rows, n = gt.shape
    return pl.pallas_call(
        functools.partial(_gate_scan_kernel, CHUNK),
        grid=(bsz,),
        in_specs=[pl.BlockSpec((None, rows, n), lambda b: (b, 0, 0))],
        out_specs=pl.BlockSpec((None, 3 * rows // 2, n), lambda b: (b, 0, 0)),
        out_shape=jax.ShapeDtypeStruct((bsz, 3 * rows // 2, n), F32),
        compiler_params=_params(("parallel",)),
        name="gate_scan",
    )(gt)


def _mlstm_kernel(emit_h, emit_state, *refs):
    fwd, bwd = refs[0:4], refs[4:8]
    ct0_ref, m0_ref = refs[8:10]
    outs = list(refs[10:])
    hf_ref, hb_ref = (outs.pop(0), outs.pop(0)) if emit_h else (None, None)
    cto_ref, mo_ref = (outs.pop(0), outs.pop(0)) if emit_state else (None, None)
    ct_scr, m_scr = outs
    i = pl.program_id(0)
    n_combo = 2 * HEADS
    bsz, _, L = fwd[0].shape

    @pl.when(i == 0)
    def _():
        ct_scr[...] = ct0_ref[...]
        m_scr[...] = m0_ref[...]

    is_fwd_row = lax.broadcasted_iota(jnp.int32, (n_combo, L), 0) < HEADS
    is_fwd_col = lax.broadcasted_iota(jnp.int32, (n_combo, 1), 0) < HEADS
    krow = lax.broadcasted_iota(jnp.int32, (8, L), 0)
    tpos = lax.broadcasted_iota(jnp.int32, (L, L), 0)
    spos = lax.broadcasted_iota(jnp.int32, (L, L), 1)
    ones_rows = jnp.ones((STATE_ROWS - HEAD_DIM, L), BF16)

    for b in range(bsz):
        sc_f, sc_b = fwd[3][b], bwd[3][b]
        pick = lambda r: jnp.where(is_fwd_row, sc_f[r * n_combo:(r + 1) * n_combo],
                                   sc_b[r * n_combo:(r + 1) * n_combo])
        last = lambda a: jnp.where(is_fwd_col, a[:, L - 1:L], a[:, 0:1])
        rowb, brow, cmax = pick(0), pick(1), pick(2)
        total, rowb_max = last(brow), last(cmax)
        m_in = m_scr[b][:, 0:1]
        mm = jnp.maximum(cmax, m_in)
        w_inter = jnp.exp(m_in - mm)
        floor = jnp.exp(-(brow + mm))
        w_state = jnp.exp(rowb - rowb_max)
        m_loc = total + rowb_max
        m_new = jnp.maximum(total + m_in, m_loc)
        a_old = jnp.exp(total + m_in - m_new)
        a_new = jnp.exp(m_loc - m_new)
        m_scr[b] = jnp.broadcast_to(m_new, (n_combo, HEAD_DIM))
        rb3, nm3 = _split3(rowb), _split3(-mm)

        for d, (src, h_ref) in enumerate(((fwd, hf_ref), (bwd, hb_ref))):
            mask = (spos <= tpos) if d == 0 else (spos >= tpos)
            for h in range(HEADS):
                c = d * HEADS + h
                sl = slice(h * HEAD_DIM, (h + 1) * HEAD_DIM)
                row = lambda a: a[c:c + 1, :]
                qt, kt, vt = src[0][b, sl, :], src[1][b, sl, :], src[2][b, sl, :]
                lhs = jnp.where(krow < 3, 1.0, jnp.where(krow == 3, row(nm3[0]), jnp.where(
                    krow == 4, row(nm3[1]), jnp.where(krow == 5, row(nm3[2]), 0.0))))
                rhs = jnp.where(krow == 0, row(rb3[0]), jnp.where(krow == 1, row(rb3[1]), jnp.where(
                    krow == 2, row(rb3[2]), jnp.where(krow < 6, 1.0, 0.0))))
                expo = _dot(lhs.astype(BF16), rhs.astype(BF16), _TN)
                dmat = jnp.where(mask, jnp.exp(expo), 0.0)
                s = (_dot(qt, kt, _TN) * dmat).astype(BF16)
                vext = jnp.concatenate([vt, ones_rows], axis=0)
                ct = ct_scr[b * n_combo + c]
                state_t = _dot(ct.astype(BF16), qt)
                if emit_h:
                    res = row(w_inter) * state_t + _dot(vext, s, _NT)
                    den = res[HEAD_DIM:HEAD_DIM + 1, :]
                    h_ref[b, sl, :] = res[:HEAD_DIM] / jnp.maximum(jnp.abs(den), row(floor))
                ktw = (kt.astype(F32) * row(w_state)).astype(BF16)
                ckv = _dot(vext, ktw, _NT)
                ct_scr[b * n_combo + c] = a_old[c:c + 1, :] * ct + a_new[c:c + 1, :] * ckv

    if emit_state:
        @pl.when(i == pl.num_programs(0) - 1)
        def _():
            cto_ref[...] = ct_scr[...]
            mo_ref[...] = m_scr[...]


def _mlstm(qt, kt, vt, sc, ct0, m0, *, emit_h):
    bsz, w, n = qt.shape
    nc = n // CHUNK

    def specs(pos_fn):
        feat = pl.BlockSpec((bsz, w, CHUNK), lambda i: (0, 0, pos_fn(i)))
        scs = pl.BlockSpec((bsz, sc.shape[1], CHUNK), lambda i: (0, 0, pos_fn(i)))
        return [feat, feat, feat, scs]

    fpos = lambda i: i
    bpos = lambda i: nc - 1 - i
    ct_spec = pl.BlockSpec(ct0.shape, lambda i: (0, 0, 0))
    m_spec = pl.BlockSpec(m0.shape, lambda i: (0, 0, 0))
    if emit_h:
        hsd = jax.ShapeDtypeStruct((bsz, w, n), F32)
        out_specs = [pl.BlockSpec((bsz, w, CHUNK), lambda i: (0, 0, fpos(i))),
                     pl.BlockSpec((bsz, w, CHUNK), lambda i: (0, 0, bpos(i)))]
        out_shape = [hsd, hsd]
    else:
        out_specs = [ct_spec, m_spec]
        out_shape = [jax.ShapeDtypeStruct(ct0.shape, F32), jax.ShapeDtypeStruct(m0.shape, F32)]
    return pl.pallas_call(
        functools.partial(_mlstm_kernel, emit_h, not emit_h),
        grid=(nc,),
        in_specs=specs(fpos) + specs(bpos) + [ct_spec, m_spec],
        out_specs=out_specs,
        out_shape=out_shape,
        scratch_shapes=[pltpu.VMEM(ct0.shape, F32), pltpu.VMEM(m0.shape, F32)],
        compiler_params=_params(("arbitrary",)),
        name="mlstm" if emit_h else "mlstm_ctx",
    )(qt, kt, vt, sc, qt, kt, vt, sc, ct0, m0)


def _mix_ffn_kernel(x1_ref, hf_ref, hb_ref, ot_ref, yp_ref, mod_ref, npre_ref, npost_ref, hn_ref,
                    wout_ref, w1a_ref, w1b_ref, w2_ref, out_ref, mixt_scr, y_scr, acc_scr):
    tm = x1_ref.shape[0]
    for h in range(HEADS):
        sl = slice(h * HEAD_DIM, (h + 1) * HEAD_DIM)
        hh = hf_ref[sl, :] + hb_ref[sl, :]
        inv = lax.rsqrt(jnp.mean(hh * hh, axis=0, keepdims=True) + EPS)
        hn = hh * inv * jnp.concatenate([hn_ref[sl, :]] * (tm // HEAD_DIM), axis=1)
        mixt_scr[sl, :] = (hn * _sigmoid(ot_ref[sl, :])).astype(BF16)
    z = _dot(mixt_scr[...], wout_ref[0:M_W, :], _TN) + _dot(yp_ref[...], wout_ref[M_W:, :])
    x2 = x1_ref[...] + (1.0 * mod_ref[0, 5:6, :]) * _rms(z, npost_ref[1:2, :])
    y = _rms(x2, npre_ref[2:3, :]) * (1.0 + mod_ref[0, 7:8, :]) + mod_ref[0, 6:7, :]
    y_scr[...] = y.astype(BF16)
    _swiglu_into(acc_scr, y_scr, w1a_ref, w1b_ref, w2_ref)
    out_ref[...] = x2 + (0.5 * mod_ref[0, 8:9, :]) * _rms(acc_scr[...], npost_ref[2:3, :])


def _mix_ffn(x1, hf, hb, ot, yp, mod_all, npre, npost, hnorm, wout, w1a, w1b, w2, *, tm):
    bsz, n, d = x1.shape
    tok = lambda w: pl.BlockSpec((None, tm, w), lambda b, i: (b, i, 0))
    feat = pl.BlockSpec((None, M_W, tm), lambda b, i: (b, 0, i))
    in_specs = [tok(d), feat, feat, feat, tok(POOL_W),
                pl.BlockSpec((1, N_MOD, d), lambda b, i: (b, 0, 0)),
                _const_spec(npre.shape), _const_spec(npost.shape), _const_spec(hnorm.shape),
                _const_spec(wout.shape), _const_spec(w1a.shape), _const_spec(w1b.shape),
                _const_spec(w2.shape)]
    return pl.pallas_call(
        _mix_ffn_kernel,
        grid=(bsz, n // tm),
        in_specs=in_specs,
        out_specs=tok(d),
        out_shape=jax.ShapeDtypeStruct((bsz, n, d), F32),
        scratch_shapes=[pltpu.VMEM((M_W, tm), BF16), pltpu.VMEM((tm, d), BF16),
                        pltpu.VMEM((tm, d), F32)],
        compiler_params=_params(("parallel", "arbitrary")),
        name="mix_ffn",
    )(x1, hf, hb, ot, yp, mod_all, npre, npost, hnorm, wout, w1a, w1b, w2)


def _ffn_weights(w_in, w_out):
    ff = w_in.shape[1] // 2
    return w_in[:, :ff].astype(BF16), w_in[:, ff:].astype(BF16), w_out.astype(BF16)


def kernel(x, c, ctx, c_ctx, w_mod, b_mod, norm_pre, norm_post, ffn_w_in, ffn_w_out, w_in, w_out,
           conv_w, conv_b, w_q, w_k, i_bias, f_bias, head_norm, pool_w, pool_scale):
    assert w_mod.shape[0] == 1, "single-layer stack"
    bsz, n, d = x.shape
    n_ctx = ctx.shape[1]
    m_cols = 3 * M_W + GATE_COLS

    rows = 8
    s_rows = jnp.zeros((rows, d), F32).at[:bsz].set(c).at[bsz].set(c_ctx)
    mod_all = _modulation(s_rows, w_mod[0], b_mod[0]).reshape(rows, N_MOD, d)

    w1a, w1b, w2 = _ffn_weights(ffn_w_in[0, 0], ffn_w_out[0, 0])
    v1a, v1b, v2 = _ffn_weights(ffn_w_in[0, 1], ffn_w_out[0, 1])
    wi = w_in[0]
    gate_w = jnp.pad(wi[:, 3 * M_W:m_cols], ((0, 0), (0, GATE_PAD - GATE_COLS)))
    win_ctx = jnp.concatenate([wi[:, :M_W], gate_w], axis=1).astype(BF16)
    win_lat = jnp.concatenate([win_ctx, wi[:, m_cols:].astype(BF16)], axis=1)
    wint_lat = wi[:, M_W:3 * M_W].T.astype(BF16)
    wint_ctx = wint_lat[:M_W]
    gbias = jnp.pad(jnp.concatenate([i_bias[0].reshape(-1), f_bias[0].reshape(-1)]),
                    (0, GATE_PAD - GATE_COLS)).reshape(1, GATE_PAD)
    npre, npost = norm_pre[0], norm_post[0]

    tm = min(512, n)
    x1, qk_l, g_l, vt_l, pool_l, ot_l = _ffn_in(x, mod_all, None, npre, npost, w1a, w1b, w2,
                                                win_lat, wint_lat, gbias, with_rest=True, tm=tm)
    qk_c, g_c, vt_c = _ffn_in(ctx, mod_all, bsz, npre, npost, w1a, w1b, w2, win_ctx, wint_ctx,
                              gbias, with_rest=False, tm=min(512, n_ctx))

    cb = conv_b[0].reshape(1, M_W)
    wqt = jnp.swapaxes(w_q[0], 1, 2).astype(BF16)
    wkt = jnp.swapaxes(w_k[0], 1, 2).astype(BF16)
    qt_l, kt_l = _qk_proj(qk_l, conv_w[0], cb, wqt, wkt, tq=min(1024, n))
    qt_c, kt_c = _qk_proj(qk_c, conv_w[0], cb, wqt, wkt, tq=min(1024, n_ctx))
    gt_l = jnp.swapaxes(g_l[:, :, :GATE_COLS], 1, 2)
    gt_c = jnp.swapaxes(g_c[:, :, :GATE_COLS], 1, 2)

    y_pool = _pool_mixer(pool_l, pool_w[0].astype(BF16), pool_scale[0].reshape(1, POOL_W),
                         tt=min(2048, n))

    ct0 = jnp.zeros((bsz * 2 * HEADS, STATE_ROWS, HEAD_DIM), F32)
    m0 = jnp.zeros((bsz, 2 * HEADS, HEAD_DIM), F32)
    ct1, m1 = _mlstm(qt_c, kt_c, vt_c, _gate_scan(gt_c), ct0, m0, emit_h=False)
    ht_f, ht_b = _mlstm(qt_l, kt_l, vt_l, _gate_scan(gt_l), ct1, m1, emit_h=True)

    hnorm = jnp.broadcast_to(head_norm[0].reshape(M_W, 1), (M_W, HEAD_DIM))
    return _mix_ffn(x1, ht_f, ht_b, ot_l, y_pool, mod_all, npre, npost, hnorm,
                    w_out[0].astype(BF16), v1a, v1b, v2, tm=tm)
```

```python
import functools

import jax
import jax.numpy as jnp
from jax import lax
from jax.experimental import pallas as pl
from jax.experimental.pallas import tpu as pltpu

F32 = jnp.float32
BF16 = jnp.bfloat16

EPS = 1e-6
N_MOD = 9
HEADS = 4
HEAD_DIM = 128
M_W = HEADS * HEAD_DIM
POOL_WINDOWS = (2, 4, 8, 16)
POOL_GW = 128
POOL_W = POOL_GW * len(POOL_WINDOWS)
GRID_W = 64
POOL_PAD = 8
CHUNK = 256
GATE_COLS = 4 * HEADS
GATE_PAD = 128
FF_CHUNK = 256
STATE_ROWS = HEAD_DIM + 16
V7X_VMEM_LIMIT = 56 * 1024 * 1024

_NT = (((1,), (1,)), ((), ()))
_TN = (((0,), (0,)), ((), ()))


def _sigmoid(x):
    return 1.0 / (1.0 + jnp.exp(-x))


def _log_sigmoid(x):
    return jnp.minimum(x, 0.0) - jnp.log(1.0 + jnp.exp(-jnp.abs(x)))


def _rms(x, g):
    ms = jnp.mean(x * x, axis=-1, keepdims=True)
    return x * lax.rsqrt(ms + EPS) * g


def _dot(a, b, dims=None):
    if dims is None:
        return jnp.dot(a, b, preferred_element_type=F32)
    return lax.dot_general(a, b, dims, preferred_element_type=F32)


def _const_spec(shape):
    nd = len(shape)
    return pl.BlockSpec(shape, lambda *_: (0,) * nd, pipeline_mode=pl.Buffered(1))


def _params(sem):
    return pltpu.CompilerParams(dimension_semantics=sem, vmem_limit_bytes=V7X_VMEM_LIMIT)


def _mod_kernel(s_ref, w_ref, b_ref, o_ref):
    s = s_ref[...]
    act = (s * _sigmoid(s)).astype(BF16)
    o_ref[...] = _dot(act, w_ref[...].astype(BF16)) + b_ref[...]


def _modulation(s_rows, w_mod, b_mod):
    rows, d = s_rows.shape
    n = w_mod.shape[1]
    tn = 1024
    return pl.pallas_call(
        _mod_kernel,
        grid=(n // tn,),
        in_specs=[pl.BlockSpec((rows, d), lambda j: (0, 0)),
                  pl.BlockSpec((d, tn), lambda j: (0, j)),
                  pl.BlockSpec((1, tn), lambda j: (0, j))],
        out_specs=pl.BlockSpec((rows, tn), lambda j: (0, j)),
        out_shape=jax.ShapeDtypeStruct((rows, n), F32),
        compiler_params=_params(("arbitrary",)),
        name="mod",
    )(s_rows, w_mod, b_mod.reshape(1, n))


def _swiglu_into(acc_ref, y_ref, w1a_ref, w1b_ref, w2_ref):
    y = y_ref[...]
    for j in range(w2_ref.shape[0] // FF_CHUNK):
        cols = slice(j * FF_CHUNK, (j + 1) * FF_CHUNK)
        a = _dot(y, w1a_ref[:, cols])
        b = _dot(y, w1b_ref[:, cols])
        h = ((a * _sigmoid(a)) * b).astype(BF16)
        part = _dot(h, w2_ref[cols, :])
        if j == 0:
            acc_ref[...] = part
        else:
            acc_ref[...] += part


def _ffn_in_kernel(with_rest, x_ref, mod_ref, npre_ref, npost_ref, w1a_ref, w1b_ref, w2_ref,
                   win_ref, wint_ref, gbias_ref, *rest):
    if with_rest:
        x1_ref, qk_ref, g_ref, vt_ref, pool_ref, ot_ref, y_scr, acc_scr = rest
    else:
        qk_ref, g_ref, vt_ref, y_scr, acc_scr = rest
    x = x_ref[...]
    y = _rms(x, npre_ref[0:1, :]) * (1.0 + mod_ref[0, 1:2, :]) + mod_ref[0, 0:1, :]
    y_scr[...] = y.astype(BF16)
    _swiglu_into(acc_scr, y_scr, w1a_ref, w1b_ref, w2_ref)
    x1 = x + (0.5 * mod_ref[0, 2:3, :]) * _rms(acc_scr[...], npost_ref[0:1, :])
    y2 = _rms(x1, npre_ref[1:2, :]) * (1.0 + mod_ref[0, 4:5, :]) + mod_ref[0, 3:4, :]
    y_scr[...] = y2.astype(BF16)
    y2b = y_scr[...]
    c0, c1 = M_W, M_W + GATE_PAD
    qk_ref[...] = _dot(y2b, win_ref[:, 0:c0])
    g_ref[...] = _dot(y2b, win_ref[:, c0:c1]) + gbias_ref[...]
    vt = _dot(wint_ref[0:M_W, :], y2b, _NT).astype(BF16)
    for c in range(vt_ref.shape[0]):
        vt_ref[c] = vt[:, c * CHUNK:(c + 1) * CHUNK]
    if with_rest:
        x1_ref[...] = x1
        pool_ref[...] = _dot(y2b, win_ref[:, c1:c1 + POOL_W])
        ot = _dot(wint_ref[M_W:2 * M_W, :], y2b, _NT)
        for c in range(ot_ref.shape[0]):
            ot_ref[c] = ot[:, c * CHUNK:(c + 1) * CHUNK]


def _ffn_in(x, mod_all, mod_row0, npre, npost, w1a, w1b, w2, win, wint, gbias, *, with_rest, tm):
    bsz, n, d = x.shape
    grid = (bsz, n // tm)
    tok = lambda w: pl.BlockSpec((None, tm, w), lambda b, i: (b, i, 0))
    feat = pl.BlockSpec((None, tm // CHUNK, M_W, CHUNK), lambda b, i: (b, i, 0, 0))
    if mod_row0 is None:
        mod_spec = pl.BlockSpec((1, N_MOD, d), lambda b, i: (b, 0, 0))
    else:
        mod_spec = pl.BlockSpec((1, N_MOD, d), lambda b, i: (mod_row0, 0, 0))
    in_specs = [tok(d), mod_spec, _const_spec(npre.shape), _const_spec(npost.shape),
                _const_spec(w1a.shape), _const_spec(w1b.shape), _const_spec(w2.shape),
                _const_spec(win.shape), _const_spec(wint.shape), _const_spec(gbias.shape)]
    sds = lambda w, dt: jax.ShapeDtypeStruct((bsz, n, w), dt)
    fsds = lambda dt: jax.ShapeDtypeStruct((bsz, n // CHUNK, M_W, CHUNK), dt)
    out_specs = [tok(M_W), tok(GATE_PAD), feat]
    out_shape = [sds(M_W, F32), sds(GATE_PAD, F32), fsds(BF16)]
    if with_rest:
        out_specs = [tok(d)] + out_specs + [tok(POOL_W), feat]
        out_shape = [sds(d, F32)] + out_shape + [sds(POOL_W, F32), fsds(F32)]
    return pl.pallas_call(
        functools.partial(_ffn_in_kernel, with_rest),
        grid=grid,
        in_specs=in_specs,
        out_specs=out_specs,
        out_shape=out_shape,
        scratch_shapes=[pltpu.VMEM((tm, d), BF16), pltpu.VMEM((tm, d), F32)],
        compiler_params=_params(("parallel", "arbitrary")),
        name="ffn_in" if with_rest else "ffn_in_ctx",
    )(x, mod_all, npre, npost, w1a, w1b, w2, win, wint, gbias)


def _qk_kernel(nt, x_ref, prev_ref, next_ref, cw_ref, cb_ref, wqt_ref, wkt_ref, qt_ref, kt_ref):
    j = pl.program_id(1)
    x = x_ref[...]
    tq = x.shape[0]
    prev_row = prev_ref[7:8, :] * jnp.where(j > 0, 1.0, 0.0)
    next_row = next_ref[0:1, :] * jnp.where(j < nt - 1, 1.0, 0.0)
    row = lax.broadcasted_iota(jnp.int32, x.shape, 0)
    up = jnp.where(row == 0, prev_row, pltpu.roll(x, 1, 0))
    dn = jnp.where(row == tq - 1, next_row, pltpu.roll(x, tq - 1, 0))
    conv = cb_ref[...] + up * cw_ref[0:1, :] + x * cw_ref[1:2, :] + dn * cw_ref[2:3, :]
    u = (conv * _sigmoid(conv)).astype(BF16)
    scale = HEAD_DIM ** -0.5
    for h in range(HEADS):
        sl = slice(h * HEAD_DIM, (h + 1) * HEAD_DIM)
        uh = u[:, sl]
        qt = _dot(wqt_ref[h], uh, _NT).astype(BF16)
        kt = (_dot(wkt_ref[h], uh, _NT) * scale).astype(BF16)
        for c in range(qt_ref.shape[0]):
            qt_ref[c, sl, :] = qt[:, c * CHUNK:(c + 1) * CHUNK]
            kt_ref[c, sl, :] = kt[:, c * CHUNK:(c + 1) * CHUNK]


def _qk_proj(qk_src, conv_w, conv_b, wqt, wkt, *, tq):
    bsz, n, w = qk_src.shape
    nt = n // tq
    blk8 = tq // 8
    last8 = n // 8 - 1
    tok = pl.BlockSpec((None, tq, w), lambda b, j: (b, j, 0))
    prev = pl.BlockSpec((None, 8, w), lambda b, j: (b, jnp.maximum(j * blk8 - 1, 0), 0))
    nxt = pl.BlockSpec((None, 8, w), lambda b, j: (b, jnp.minimum((j + 1) * blk8, last8), 0))
    feat = pl.BlockSpec((None, tq // CHUNK, w, CHUNK), lambda b, j: (b, j, 0, 0))
    out = jax.ShapeDtypeStruct((bsz, n // CHUNK, w, CHUNK), BF16)
    return pl.pallas_call(
        functools.partial(_qk_kernel, nt),
        grid=(bsz, nt),
        in_specs=[tok, prev, nxt, _const_spec(conv_w.shape), _const_spec(conv_b.shape),
                  _const_spec(wqt.shape), _const_spec(wkt.shape)],
        out_specs=[feat, feat],
        out_shape=[out, out],
        compiler_params=_params(("parallel", "arbitrary")),
        name="qk",
    )(qk_src, qk_src, qk_src, conv_w, conv_b, wqt, wkt)


def _pool_kernel(nt, grid_h, x_ref, prev_ref, next_ref, wp_ref, ps_ref, y_ref, xs_ref, *bufs):
    j = pl.program_id(1)
    tt = x_ref.shape[0]
    halo = prev_ref.shape[0]
    halo_rows = halo // GRID_W
    rows = tt // GRID_W
    d0, d1 = POOL_PAD, POOL_PAD + GRID_W
    xs_ref[0:halo, :] = prev_ref[...] * jnp.where(j > 0, 1.0, 0.0)
    xs_ref[halo:halo + tt, :] = x_ref[...]
    xs_ref[halo + tt:halo + tt + halo, :] = next_ref[...] * jnp.where(j < nt - 1, 1.0, 0.0)
    zpad = jnp.zeros((rows, POOL_PAD, POOL_GW), F32)
    for buf in bufs:
        buf[:, 0:d0, :] = zpad
        buf[:, d1:d1 + POOL_PAD, :] = zpad
    c = lax.broadcasted_iota(jnp.int32, (1, GRID_W, POOL_GW), 1)
    r = j * rows + lax.broadcasted_iota(jnp.int32, (rows, 1, POOL_GW), 0)

    def shifted_sum(src, dst, k):
        dst[:, d0:d1, :] = src[:, d0:d1, :] + src[:, d0 + k:d1 + k, :]

    for g, win in enumerate(POOL_WINDOWS):
        lo, hi = win // 2, win - 1 - win // 2
        sl = slice(g * POOL_GW, (g + 1) * POOL_GW)
        first = (halo_rows - lo) * GRID_W
        vs = xs_ref[first:first + (rows + win - 1) * GRID_W, sl]
        span = 1
        while span < win:
            vs = vs[:-span * GRID_W] + vs[span * GRID_W:]
            span *= 2
        a, b, cc, dd = bufs
        a[:, d0:d1, :] = vs.reshape(rows, GRID_W, POOL_GW)
        trail, lead, spare_t, spare_l, k = a, a, b, dd, 1
        while k < lo:
            shifted_sum(trail, spare_t, -k)
            shifted_sum(lead, spare_l, k)
            trail, spare_t = spare_t, (cc if spare_t is b else b)
            lead, spare_l = spare_l, (a if spare_l is dd else dd)
            k *= 2
        tot = trail[:, d0 - 1:d1 - 1, :] + lead[:, d0:d1, :]
        cnt_c = jnp.minimum(c + hi, GRID_W - 1) - jnp.maximum(c - lo, 0) + 1
        cnt_r = jnp.minimum(r + hi, grid_h - 1) - jnp.maximum(r - lo, 0) + 1
        pg = tot * (1.0 / cnt_c.astype(F32)) * (1.0 / cnt_r.astype(F32))
        dlt = (pg.reshape(tt, POOL_GW) - x_ref[:, sl]).astype(BF16)
        y_ref[:, sl] = (_dot(dlt, wp_ref[g]) * ps_ref[:, sl]).astype(BF16)


def _pool_mixer(u, wp, ps, *, tt):
    bsz, n, w = u.shape
    nt = n // tt
    halo = 8 * GRID_W
    hb = tt // halo
    last = n // halo - 1
    tok = pl.BlockSpec((None, tt, w), lambda b, j: (b, j, 0))
    prev = pl.BlockSpec((None, halo, w), lambda b, j: (b, jnp.maximum(j * hb - 1, 0), 0))
    nxt = pl.BlockSpec((None, halo, w), lambda b, j: (b, jnp.minimum((j + 1) * hb, last), 0))
    return pl.pallas_call(
        functools.partial(_pool_kernel, nt, n // GRID_W),
        grid=(bsz, nt),
        in_specs=[tok, prev, nxt, _const_spec(wp.shape), _const_spec(ps.shape)],
        out_specs=tok,
        out_shape=jax.ShapeDtypeStruct((bsz, n, w), BF16),
        scratch_shapes=[pltpu.VMEM((tt + 2 * halo, w), F32)]
        + [pltpu.VMEM((tt // GRID_W, GRID_W + 2 * POOL_PAD, POOL_GW), F32)] * 4,
        compiler_params=_params(("parallel", "arbitrary")),
        name="pool",
    )(u, u, u, wp, ps)


def _split3(x):
    hi = x.astype(BF16).astype(F32)
    r = x - hi
    mid = r.astype(BF16).astype(F32)
    lo = (r - mid).astype(BF16).astype(F32)
    return hi, mid, lo


def _gate_scan_kernel(chunk, gt_ref, o_ref):
    n_combo = 2 * HEADS
    g = gt_ref[...]
    n = g.shape[1]
    pos = jnp.bitwise_and(lax.broadcasted_iota(jnp.int32, (n_combo, n), 1), chunk - 1)
    is_fwd_row = lax.broadcasted_iota(jnp.int32, (n_combo, n), 0) < HEADS

    def scan(x, op, ident):
        sft = 1
        while sft < chunk:
            from_l = jnp.where(pos >= sft, pltpu.roll(x, sft, 1), ident)
            from_r = jnp.where(pos < chunk - sft, pltpu.roll(x, n - sft, 1), ident)
            x = op(x, jnp.where(is_fwd_row, from_l, from_r))
            sft *= 2
        return x

    brow = scan(_log_sigmoid(g[n_combo:]), jnp.add, 0.0)
    rowb = g[:n_combo] - brow
    o_ref[0:n_combo, :] = rowb
    o_ref[n_combo:2 * n_combo, :] = brow
    o_ref[2 * n_combo:, :] = scan(rowb, jnp.maximum, -jnp.inf)


def _gate_scan(gt):
    bsz, rows, n = gt.shape
    return pl.pallas_call(
        functools.partial(_gate_scan_kernel, CHUNK),
        grid=(bsz,),
        in_specs=[pl.BlockSpec((None, rows, n), lambda b: (b, 0, 0))],
        out_specs=pl.BlockSpec((None, 3 * rows // 2, n), lambda b: (b, 0, 0)),
        out_shape=jax.ShapeDtypeStruct((bsz, 3 * rows // 2, n), F32),
        compiler_params=_params(("parallel",)),
        name="gate_scan",
    )(gt)


def _mlstm_kernel(emit_h, emit_state, *refs):
    fwd, bwd = refs[0:4], refs[4:8]
    ct0_ref, m0_ref = refs[8:10]
    outs = list(refs[10:])
    hf_ref, hb_ref = (outs.pop(0), outs.pop(0)) if emit_h else (None, None)
    cto_ref, mo_ref = (outs.pop(0), outs.pop(0)) if emit_state else (None, None)
    ct_scr, m_scr = outs
    i = pl.program_id(0)
    n_combo = 2 * HEADS
    bsz, _, L = fwd[0].shape

    @pl.when(i == 0)
    def _():
        ct_scr[...] = ct0_ref[...]
        m_scr[...] = m0_ref[...]

    is_fwd_row = lax.broadcasted_iota(jnp.int32, (n_combo, L), 0) < HEADS
    is_fwd_col = lax.broadcasted_iota(jnp.int32, (n_combo, 1), 0) < HEADS
    krow = lax.broadcasted_iota(jnp.int32, (8, L), 0)
    tpos = lax.broadcasted_iota(jnp.int32, (L, L), 0)
    spos = lax.broadcasted_iota(jnp.int32, (L, L), 1)
    masks = (spos <= tpos, spos >= tpos)
    ones_rows = jnp.ones((STATE_ROWS - HEAD_DIM, L), BF16)

    def position_rows(b):
        sc_f, sc_b = fwd[3][b], bwd[3][b]
        pick = lambda r: jnp.where(is_fwd_row, sc_f[r * n_combo:(r + 1) * n_combo],
                                   sc_b[r * n_combo:(r + 1) * n_combo])
        last = lambda a: jnp.where(is_fwd_col, a[:, L - 1:L], a[:, 0:1])
        rowb, brow, cmax = pick(0), pick(1), pick(2)
        total, rowb_max = last(brow), last(cmax)
        m_in = m_scr[b][:, 0:1]
        mm = jnp.maximum(cmax, m_in)
        m_loc = total + rowb_max
        m_new = jnp.maximum(total + m_in, m_loc)
        m_scr[b] = jnp.broadcast_to(m_new, (n_combo, HEAD_DIM))
        return dict(w_inter=jnp.exp(m_in - mm), floor=jnp.exp(-(brow + mm)),
                    w_state=jnp.exp(rowb - rowb_max), a_old=jnp.exp(total + m_in - m_new),
                    a_new=jnp.exp(m_loc - m_new), rb3=_split3(rowb), nm3=_split3(-mm))

    def front(b, d, h, pr):
        c = d * HEADS + h
        sl = slice(h * HEAD_DIM, (h + 1) * HEAD_DIM)
        row = lambda a: a[c:c + 1, :]
        src = fwd if d == 0 else bwd
        qt, kt, vt = src[0][b, sl, :], src[1][b, sl, :], src[2][b, sl, :]
        vext = jnp.concatenate([vt, ones_rows], axis=0)
        ct = ct_scr[b * n_combo + c]
        out = dict(b=b, d=d, sl=sl, vext=vext, w_inter=row(pr["w_inter"]), floor=row(pr["floor"]))
        if emit_h:
            nm3, rb3 = pr["nm3"], pr["rb3"]
            lhs = jnp.where(krow < 3, 1.0, jnp.where(krow == 3, row(nm3[0]), jnp.where(
                krow == 4, row(nm3[1]), jnp.where(krow == 5, row(nm3[2]), 0.0))))
            rhs = jnp.where(krow == 0, row(rb3[0]), jnp.where(krow == 1, row(rb3[1]), jnp.where(
                krow == 2, row(rb3[2]), jnp.where(krow < 6, 1.0, 0.0))))
            out["expo"] = _dot(lhs.astype(BF16), rhs.astype(BF16), _TN)
            out["s0"] = _dot(qt, kt, _TN)
            out["state_t"] = _dot(ct.astype(BF16), qt)
        ktw = (kt.astype(F32) * row(pr["w_state"])).astype(BF16)
        ckv = _dot(vext, ktw, _NT)
        ct_scr[b * n_combo + c] = pr["a_old"][c:c + 1, :] * ct + pr["a_new"][c:c + 1, :] * ckv
        return out

    def back(f):
        dmat = jnp.where(masks[f["d"]], jnp.exp(f["expo"]), 0.0)
        s = (f["s0"] * dmat).astype(BF16)
        res = f["w_inter"] * f["state_t"] + _dot(f["vext"], s, _NT)
        den = res[HEAD_DIM:HEAD_DIM + 1, :]
        h_ref = hf_ref if f["d"] == 0 else hb_ref
        h_ref[f["b"], f["sl"], :] = res[:HEAD_DIM] / jnp.maximum(jnp.abs(den), f["floor"])

    pending = None
    for b in range(bsz):
        pr = position_rows(b)
        for d in range(2):
            for h in range(HEADS):
                cur = front(b, d, h, pr)
                if emit_h and pending is not None:
                    back(pending)
                pending = cur
    if emit_h:
        back(pending)

    if emit_state:
        @pl.when(i == pl.num_programs(0) - 1)
        def _():
            cto_ref[...] = ct_scr[...]
            mo_ref[...] = m_scr[...]


def _mlstm(qt, kt, vt, sc, ct0, m0, *, emit_h):
    bsz, nc, w, _ = qt.shape
    fpos = lambda i: i
    bpos = lambda i: nc - 1 - i

    def feat(pos_fn):
        return pl.BlockSpec((bsz, None, w, CHUNK), lambda i: (0, pos_fn(i), 0, 0))

    def specs(pos_fn):
        scs = pl.BlockSpec((bsz, sc.shape[1], CHUNK), lambda i: (0, 0, pos_fn(i)))
        return [feat(pos_fn)] * 3 + [scs]

    ct_spec = pl.BlockSpec(ct0.shape, lambda i: (0, 0, 0))
    m_spec = pl.BlockSpec(m0.shape, lambda i: (0, 0, 0))
    if emit_h:
        hsd = jax.ShapeDtypeStruct((bsz, nc, w, CHUNK), F32)
        out_specs = [feat(fpos), feat(bpos)]
        out_shape = [hsd, hsd]
    else:
        out_specs = [ct_spec, m_spec]
        out_shape = [jax.ShapeDtypeStruct(ct0.shape, F32), jax.ShapeDtypeStruct(m0.shape, F32)]
    return pl.pallas_call(
        functools.partial(_mlstm_kernel, emit_h, not emit_h),
        grid=(nc,),
        in_specs=specs(fpos) + specs(bpos) + [ct_spec, m_spec],
        out_specs=out_specs,
        out_shape=out_shape,
        scratch_shapes=[pltpu.VMEM(ct0.shape, F32), pltpu.VMEM(m0.shape, F32)],
        compiler_params=_params(("arbitrary",)),
        name="mlstm" if emit_h else "mlstm_ctx",
    )(qt, kt, vt, sc, qt, kt, vt, sc, ct0, m0)


def _mix_ffn_kernel(x1_ref, hf_ref, hb_ref, ot_ref, yp_ref, mod_ref, npre_ref, npost_ref, hn_ref,
                    wout_ref, w1a_ref, w1b_ref, w2_ref, out_ref, mixt_scr, y_scr, acc_scr):
    hnorm = jnp.concatenate([hn_ref[...]] * (CHUNK // HEAD_DIM), axis=1)
    for c in range(hf_ref.shape[0]):
        for h in range(HEADS):
            sl = slice(h * HEAD_DIM, (h + 1) * HEAD_DIM)
            hh = hf_ref[c, sl, :] + hb_ref[c, sl, :]
            inv = lax.rsqrt(jnp.mean(hh * hh, axis=0, keepdims=True) + EPS)
            gated = hh * inv * hnorm[sl, :] * _sigmoid(ot_ref[c, sl, :])
            mixt_scr[sl, c * CHUNK:(c + 1) * CHUNK] = gated.astype(BF16)
    z = _dot(mixt_scr[...], wout_ref[0:M_W, :], _TN) + _dot(yp_ref[...], wout_ref[M_W:, :])
    x2 = x1_ref[...] + (1.0 * mod_ref[0, 5:6, :]) * _rms(z, npost_ref[1:2, :])
    y = _rms(x2, npre_ref[2:3, :]) * (1.0 + mod_ref[0, 7:8, :]) + mod_ref[0, 6:7, :]
    y_scr[...] = y.astype(BF16)
    _swiglu_into(acc_scr, y_scr, w1a_ref, w1b_ref, w2_ref)
    out_ref[...] = x2 + (0.5 * mod_ref[0, 8:9, :]) * _rms(acc_scr[...], npost_ref[2:3, :])


def _mix_ffn(x1, hf, hb, ot, yp, mod_all, npre, npost, hnorm, wout, w1a, w1b, w2, *, tm):
    bsz, n, d = x1.shape
    tok = lambda w: pl.BlockSpec((None, tm, w), lambda b, i: (b, i, 0))
    feat = pl.BlockSpec((None, tm // CHUNK, M_W, CHUNK), lambda b, i: (b, i, 0, 0))
    in_specs = [tok(d), feat, feat, feat, tok(POOL_W),
                pl.BlockSpec((1, N_MOD, d), lambda b, i: (b, 0, 0)),
                _const_spec(npre.shape), _const_spec(npost.shape), _const_spec(hnorm.shape),
                _const_spec(wout.shape), _const_spec(w1a.shape), _const_spec(w1b.shape),
                _const_spec(w2.shape)]
    return pl.pallas_call(
        _mix_ffn_kernel,
        grid=(bsz, n // tm),
        in_specs=in_specs,
        out_specs=tok(d),
        out_shape=jax.ShapeDtypeStruct((bsz, n, d), F32),
        scratch_shapes=[pltpu.VMEM((M_W, tm), BF16), pltpu.VMEM((tm, d), BF16),
                        pltpu.VMEM((tm, d), F32)],
        compiler_params=_params(("parallel", "arbitrary")),
        name="mix_ffn",
    )(x1, hf, hb, ot, yp, mod_all, npre, npost, hnorm, wout, w1a, w1b, w2)


def _ffn_weights(w_in, w_out):
    ff = w_in.shape[1] // 2
    return w_in[:, :ff].astype(BF16), w_in[:, ff:].astype(BF16), w_out.astype(BF16)


def kernel(x, c, ctx, c_ctx, w_mod, b_mod, norm_pre, norm_post, ffn_w_in, ffn_w_out, w_in, w_out,
           conv_w, conv_b, w_q, w_k, i_bias, f_bias, head_norm, pool_w, pool_scale):
    assert w_mod.shape[0] == 1, "single-layer stack"
    bsz, n, d = x.shape
    n_ctx = ctx.shape[1]
    m_cols = 3 * M_W + GATE_COLS

    rows = 8
    s_rows = jnp.zeros((rows, d), F32).at[:bsz].set(c).at[bsz].set(c_ctx)
    mod_all = _modulation(s_rows, w_mod[0], b_mod[0]).reshape(rows, N_MOD, d)

    w1a, w1b, w2 = _ffn_weights(ffn_w_in[0, 0], ffn_w_out[0, 0])
    v1a, v1b, v2 = _ffn_weights(ffn_w_in[0, 1], ffn_w_out[0, 1])
    wi = w_in[0]
    gate_w = jnp.pad(wi[:, 3 * M_W:m_cols], ((0, 0), (0, GATE_PAD - GATE_COLS)))
    win_ctx = jnp.concatenate([wi[:, :M_W], gate_w], axis=1).astype(BF16)
    win_lat = jnp.concatenate([win_ctx, wi[:, m_cols:].astype(BF16)], axis=1)
    wint_lat = wi[:, M_W:3 * M_W].T.astype(BF16)
    wint_ctx = wint_lat[:M_W]
    gbias = jnp.pad(jnp.concatenate([i_bias[0].reshape(-1), f_bias[0].reshape(-1)]),
                    (0, GATE_PAD - GATE_COLS)).reshape(1, GATE_PAD)
    npre, npost = norm_pre[0], norm_post[0]

    tm = min(512, n)
    x1, qk_l, g_l, vt_l, pool_l, ot_l = _ffn_in(x, mod_all, None, npre, npost, w1a, w1b, w2,
                                                win_lat, wint_lat, gbias, with_rest=True, tm=tm)
    qk_c, g_c, vt_c = _ffn_in(ctx, mod_all, bsz, npre, npost, w1a, w1b, w2, win_ctx, wint_ctx,
                              gbias, with_rest=False, tm=min(512, n_ctx))

    cb = conv_b[0].reshape(1, M_W)
    wqt = jnp.swapaxes(w_q[0], 1, 2).astype(BF16)
    wkt = jnp.swapaxes(w_k[0], 1, 2).astype(BF16)
    qt_l, kt_l = _qk_proj(qk_l, conv_w[0], cb, wqt, wkt, tq=min(1024, n))
    qt_c, kt_c = _qk_proj(qk_c, conv_w[0], cb, wqt, wkt, tq=min(1024, n_ctx))
    gt_l = jnp.swapaxes(g_l[:, :, :GATE_COLS], 1, 2)
    gt_c = jnp.swapaxes(g_c[:, :, :GATE_COLS], 1, 2)

    y_pool = _pool_mixer(pool_l, pool_w[0].astype(BF16), pool_scale[0].reshape(1, POOL_W),
                         tt=min(2048, n))

    ct0 = jnp.zeros((bsz * 2 * HEADS, STATE_ROWS, HEAD_DIM), F32)
    m0 = jnp.zeros((bsz, 2 * HEADS, HEAD_DIM), F32)
    ct1, m1 = _mlstm(qt_c, kt_c, vt_c, _gate_scan(gt_c), ct0, m0, emit_h=False)
    ht_f, ht_b = _mlstm(qt_l, kt_l, vt_l, _gate_scan(gt_l), ct1, m1, emit_h=True)

    hnorm = jnp.broadcast_to(head_norm[0].reshape(M_W, 1), (M_W, HEAD_DIM))
    return _mix_ffn(x1, ht_f, ht_b, ot_l, y_pool, mod_all, npre, npost, hnorm,
                    w_out[0].astype(BF16), v1a, v1b, v2, tm=tm)
```

```python
import functools

import jax
import jax.numpy as jnp
from jax import lax
from jax.experimental import pallas as pl
from jax.experimental.pallas import tpu as pltpu

F32 = jnp.float32
BF16 = jnp.bfloat16

EPS = 1e-6
N_MOD = 9
HEADS = 4
HEAD_DIM = 128
M_W = HEADS * HEAD_DIM
POOL_WINDOWS = (2, 4, 8, 16)
POOL_GW = 128
POOL_W = POOL_GW * len(POOL_WINDOWS)
GRID_W = 64
POOL_PAD = 8
CHUNK = 256
GATE_COLS = 4 * HEADS
GATE_PAD = 128
FF_CHUNK = 256
STATE_ROWS = HEAD_DIM + 16
MLSTM_CHUNKS_PER_STEP = 2
V7X_VMEM_LIMIT = 56 * 1024 * 1024

_NT = (((1,), (1,)), ((), ()))
_TN = (((0,), (0,)), ((), ()))


def _sigmoid(x):
    return 1.0 / (1.0 + jnp.exp(-x))


def _log_sigmoid(x):
    return jnp.minimum(x, 0.0) - jnp.log(1.0 + jnp.exp(-jnp.abs(x)))


def _rms(x, g):
    ms = jnp.mean(x * x, axis=-1, keepdims=True)
    return x * lax.rsqrt(ms + EPS) * g


def _dot(a, b, dims=None):
    if dims is None:
        return jnp.dot(a, b, preferred_element_type=F32)
    return lax.dot_general(a, b, dims, preferred_element_type=F32)


def _const_spec(shape):
    nd = len(shape)
    return pl.BlockSpec(shape, lambda *_: (0,) * nd, pipeline_mode=pl.Buffered(1))


def _params(sem):
    return pltpu.CompilerParams(dimension_semantics=sem, vmem_limit_bytes=V7X_VMEM_LIMIT)


def _mod_kernel(s_ref, w_ref, b_ref, o_ref):
    s = s_ref[...]
    act = (s * _sigmoid(s)).astype(BF16)
    o_ref[...] = _dot(act, w_ref[...].astype(BF16)) + b_ref[...]


def _modulation(s_rows, w_mod, b_mod):
    rows, d = s_rows.shape
    n = w_mod.shape[1]
    tn = 1024
    return pl.pallas_call(
        _mod_kernel,
        grid=(n // tn,),
        in_specs=[pl.BlockSpec((rows, d), lambda j: (0, 0)),
                  pl.BlockSpec((d, tn), lambda j: (0, j)),
                  pl.BlockSpec((1, tn), lambda j: (0, j))],
        out_specs=pl.BlockSpec((rows, tn), lambda j: (0, j)),
        out_shape=jax.ShapeDtypeStruct((rows, n), F32),
        compiler_params=_params(("arbitrary",)),
        name="mod",
    )(s_rows, w_mod, b_mod.reshape(1, n))


def _swiglu_into(acc_ref, y_ref, w1a_ref, w1b_ref, w2_ref):
    y = y_ref[...]
    for j in range(w2_ref.shape[0] // FF_CHUNK):
        cols = slice(j * FF_CHUNK, (j + 1) * FF_CHUNK)
        a = _dot(y, w1a_ref[:, cols])
        b = _dot(y, w1b_ref[:, cols])
        h = ((a * _sigmoid(a)) * b).astype(BF16)
        part = _dot(h, w2_ref[cols, :])
        if j == 0:
            acc_ref[...] = part
        else:
            acc_ref[...] += part


def _ffn_in_kernel(with_rest, x_ref, mod_ref, npre_ref, npost_ref, w1a_ref, w1b_ref, w2_ref,
                   win_ref, wint_ref, gbias_ref, *rest):
    if with_rest:
        x1_ref, qk_ref, g_ref, vt_ref, pool_ref, ot_ref, y_scr, acc_scr = rest
    else:
        qk_ref, g_ref, vt_ref, y_scr, acc_scr = rest
    x = x_ref[...]
    y = _rms(x, npre_ref[0:1, :]) * (1.0 + mod_ref[0, 1:2, :]) + mod_ref[0, 0:1, :]
    y_scr[...] = y.astype(BF16)
    _swiglu_into(acc_scr, y_scr, w1a_ref, w1b_ref, w2_ref)
    x1 = x + (0.5 * mod_ref[0, 2:3, :]) * _rms(acc_scr[...], npost_ref[0:1, :])
    y2 = _rms(x1, npre_ref[1:2, :]) * (1.0 + mod_ref[0, 4:5, :]) + mod_ref[0, 3:4, :]
    y_scr[...] = y2.astype(BF16)
    y2b = y_scr[...]
    c0, c1 = M_W, M_W + GATE_PAD
    qk_ref[...] = _dot(y2b, win_ref[:, 0:c0])
    g_ref[...] = _dot(y2b, win_ref[:, c0:c1]) + gbias_ref[...]
    vt = _dot(wint_ref[0:M_W, :], y2b, _NT).astype(BF16)
    for c in range(vt_ref.shape[0]):
        vt_ref[c] = vt[:, c * CHUNK:(c + 1) * CHUNK]
    if with_rest:
        x1_ref[...] = x1
        pool_ref[...] = _dot(y2b, win_ref[:, c1:c1 + POOL_W])
        ot = _dot(wint_ref[M_W:2 * M_W, :], y2b, _NT)
        for c in range(ot_ref.shape[0]):
            ot_ref[c] = ot[:, c * CHUNK:(c + 1) * CHUNK]


def _ffn_in(x, mod_all, mod_row0, npre, npost, w1a, w1b, w2, win, wint, gbias, *, with_rest, tm):
    bsz, n, d = x.shape
    grid = (bsz, n // tm)
    tok = lambda w: pl.BlockSpec((None, tm, w), lambda b, i: (b, i, 0))
    feat = pl.BlockSpec((None, tm // CHUNK, M_W, CHUNK), lambda b, i: (b, i, 0, 0))
    if mod_row0 is None:
        mod_spec = pl.BlockSpec((1, N_MOD, d), lambda b, i: (b, 0, 0))
    else:
        mod_spec = pl.BlockSpec((1, N_MOD, d), lambda b, i: (mod_row0, 0, 0))
    in_specs = [tok(d), mod_spec, _const_spec(npre.shape), _const_spec(npost.shape),
                _const_spec(w1a.shape), _const_spec(w1b.shape), _const_spec(w2.shape),
                _const_spec(win.shape), _const_spec(wint.shape), _const_spec(gbias.shape)]
    sds = lambda w, dt: jax.ShapeDtypeStruct((bsz, n, w), dt)
    fsds = lambda dt: jax.ShapeDtypeStruct((bsz, n // CHUNK, M_W, CHUNK), dt)
    out_specs = [tok(M_W), tok(GATE_PAD), feat]
    out_shape = [sds(M_W, F32), sds(GATE_PAD, F32), fsds(BF16)]
    if with_rest:
        out_specs = [tok(d)] + out_specs + [tok(POOL_W), feat]
        out_shape = [sds(d, F32)] + out_shape + [sds(POOL_W, F32), fsds(F32)]
    return pl.pallas_call(
        functools.partial(_ffn_in_kernel, with_rest),
        grid=grid,
        in_specs=in_specs,
        out_specs=out_specs,
        out_shape=out_shape,
        scratch_shapes=[pltpu.VMEM((tm, d), BF16), pltpu.VMEM((tm, d), F32)],
        compiler_params=_params(("parallel", "arbitrary")),
        name="ffn_in" if with_rest else "ffn_in_ctx",
    )(x, mod_all, npre, npost, w1a, w1b, w2, win, wint, gbias)


def _qk_kernel(nt, x_ref, prev_ref, next_ref, cw_ref, cb_ref, wqt_ref, wkt_ref, qt_ref, kt_ref):
    j = pl.program_id(1)
    x = x_ref[...]
    tq = x.shape[0]
    prev_row = prev_ref[7:8, :] * jnp.where(j > 0, 1.0, 0.0)
    next_row = next_ref[0:1, :] * jnp.where(j < nt - 1, 1.0, 0.0)
    row = lax.broadcasted_iota(jnp.int32, x.shape, 0)
    up = jnp.where(row == 0, prev_row, pltpu.roll(x, 1, 0))
    dn = jnp.where(row == tq - 1, next_row, pltpu.roll(x, tq - 1, 0))
    conv = cb_ref[...] + up * cw_ref[0:1, :] + x * cw_ref[1:2, :] + dn * cw_ref[2:3, :]
    u = (conv * _sigmoid(conv)).astype(BF16)
    scale = HEAD_DIM ** -0.5
    for h in range(HEADS):
        sl = slice(h * HEAD_DIM, (h + 1) * HEAD_DIM)
        uh = u[:, sl]
        qt = _dot(wqt_ref[h], uh, _NT).astype(BF16)
        kt = (_dot(wkt_ref[h], uh, _NT) * scale).astype(BF16)
        for c in range(qt_ref.shape[0]):
            qt_ref[c, sl, :] = qt[:, c * CHUNK:(c + 1) * CHUNK]
            kt_ref[c, sl, :] = kt[:, c * CHUNK:(c + 1) * CHUNK]


def _qk_proj(qk_src, conv_w, conv_b, wqt, wkt, *, tq):
    bsz, n, w = qk_src.shape
    nt = n // tq
    blk8 = tq // 8
    last8 = n // 8 - 1
    tok = pl.BlockSpec((None, tq, w), lambda b, j: (b, j, 0))
    prev = pl.BlockSpec((None, 8, w), lambda b, j: (b, jnp.maximum(j * blk8 - 1, 0), 0))
    nxt = pl.BlockSpec((None, 8, w), lambda b, j: (b, jnp.minimum((j + 1) * blk8, last8), 0))
    feat = pl.BlockSpec((None, tq // CHUNK, w, CHUNK), lambda b, j: (b, j, 0, 0))
    out = jax.ShapeDtypeStruct((bsz, n // CHUNK, w, CHUNK), BF16)
    return pl.pallas_call(
        functools.partial(_qk_kernel, nt),
        grid=(bsz, nt),
        in_specs=[tok, prev, nxt, _const_spec(conv_w.shape), _const_spec(conv_b.shape),
                  _const_spec(wqt.shape), _const_spec(wkt.shape)],
        out_specs=[feat, feat],
        out_shape=[out, out],
        compiler_params=_params(("parallel", "arbitrary")),
        name="qk",
    )(qk_src, qk_src, qk_src, conv_w, conv_b, wqt, wkt)


def _pool_kernel(nt, grid_h, x_ref, prev_ref, next_ref, wp_ref, ps_ref, y_ref, xs_ref, *bufs):
    j = pl.program_id(1)
    tt = x_ref.shape[0]
    halo = prev_ref.shape[0]
    halo_rows = halo // GRID_W
    rows = tt // GRID_W
    d0, d1 = POOL_PAD, POOL_PAD + GRID_W
    xs_ref[0:halo, :] = prev_ref[...] * jnp.where(j > 0, 1.0, 0.0)
    xs_ref[halo:halo + tt, :] = x_ref[...]
    xs_ref[halo + tt:halo + tt + halo, :] = next_ref[...] * jnp.where(j < nt - 1, 1.0, 0.0)
    zpad = jnp.zeros((rows, POOL_PAD, POOL_GW), F32)
    for buf in bufs:
        buf[:, 0:d0, :] = zpad
        buf[:, d1:d1 + POOL_PAD, :] = zpad
    c = lax.broadcasted_iota(jnp.int32, (1, GRID_W, POOL_GW), 1)
    r = j * rows + lax.broadcasted_iota(jnp.int32, (rows, 1, POOL_GW), 0)

    def shifted_sum(src, dst, k):
        dst[:, d0:d1, :] = src[:, d0:d1, :] + src[:, d0 + k:d1 + k, :]

    for g, win in enumerate(POOL_WINDOWS):
        lo, hi = win // 2, win - 1 - win // 2
        sl = slice(g * POOL_GW, (g + 1) * POOL_GW)
        first = (halo_rows - lo) * GRID_W
        vs = xs_ref[first:first + (rows + win - 1) * GRID_W, sl]
        span = 1
        while span < win:
            vs = vs[:-span * GRID_W] + vs[span * GRID_W:]
            span *= 2
        a, b, cc, dd = bufs
        a[:, d0:d1, :] = vs.reshape(rows, GRID_W, POOL_GW)
        trail, lead, spare_t, spare_l, k = a, a, b, dd, 1
        while k < lo:
            shifted_sum(trail, spare_t, -k)
            shifted_sum(lead, spare_l, k)
            trail, spare_t = spare_t, (cc if spare_t is b else b)
            lead, spare_l = spare_l, (a if spare_l is dd else dd)
            k *= 2
        tot = trail[:, d0 - 1:d1 - 1, :] + lead[:, d0:d1, :]
        cnt_c = jnp.minimum(c + hi, GRID_W - 1) - jnp.maximum(c - lo, 0) + 1
        cnt_r = jnp.minimum(r + hi, grid_h - 1) - jnp.maximum(r - lo, 0) + 1
        pg = tot * (1.0 / cnt_c.astype(F32)) * (1.0 / cnt_r.astype(F32))
        dlt = (pg.reshape(tt, POOL_GW) - x_ref[:, sl]).astype(BF16)
        y_ref[:, sl] = (_dot(dlt, wp_ref[g]) * ps_ref[:, sl]).astype(BF16)


def _pool_mixer(u, wp, ps, *, tt):
    bsz, n, w = u.shape
    nt = n // tt
    halo = 8 * GRID_W
    hb = tt // halo
    last = n // halo - 1
    tok = pl.BlockSpec((None, tt, w), lambda b, j: (b, j, 0))
    prev = pl.BlockSpec((None, halo, w), lambda b, j: (b, jnp.maximum(j * hb - 1, 0), 0))
    nxt = pl.BlockSpec((None, halo, w), lambda b, j: (b, jnp.minimum((j + 1) * hb, last), 0))
    return pl.pallas_call(
        functools.partial(_pool_kernel, nt, n // GRID_W),
        grid=(bsz, nt),
        in_specs=[tok, prev, nxt, _const_spec(wp.shape), _const_spec(ps.shape)],
        out_specs=tok,
        out_shape=jax.ShapeDtypeStruct((bsz, n, w), BF16),
        scratch_shapes=[pltpu.VMEM((tt + 2 * halo, w), F32)]
        + [pltpu.VMEM((tt // GRID_W, GRID_W + 2 * POOL_PAD, POOL_GW), F32)] * 4,
        compiler_params=_params(("parallel", "arbitrary")),
        name="pool",
    )(u, u, u, wp, ps)


def _split3(x):
    hi = x.astype(BF16).astype(F32)
    r = x - hi
    mid = r.astype(BF16).astype(F32)
    lo = (r - mid).astype(BF16).astype(F32)
    return hi, mid, lo


def _gate_scan_kernel(chunk, gt_ref, o_ref):
    n_combo = 2 * HEADS
    g = gt_ref[...]
    n = g.shape[1]
    pos = jnp.bitwise_and(lax.broadcasted_iota(jnp.int32, (n_combo, n), 1), chunk - 1)
    is_fwd_row = lax.broadcasted_iota(jnp.int32, (n_combo, n), 0) < HEADS

    def scan(x, op, ident):
        sft = 1
        while sft < chunk:
            from_l = jnp.where(pos >= sft, pltpu.roll(x, sft, 1), ident)
            from_r = jnp.where(pos < chunk - sft, pltpu.roll(x, n - sft, 1), ident)
            x = op(x, jnp.where(is_fwd_row, from_l, from_r))
            sft *= 2
        return x

    brow = scan(_log_sigmoid(g[n_combo:]), jnp.add, 0.0)
    rowb = g[:n_combo] - brow
    o_ref[0:n_combo, :] = rowb
    o_ref[n_combo:2 * n_combo, :] = brow
    o_ref[2 * n_combo:, :] = scan(rowb, jnp.maximum, -jnp.inf)


def _gate_scan(gt):
    bsz, rows, n = gt.shape
    return pl.pallas_call(
        functools.partial(_gate_scan_kernel, CHUNK),
        grid=(bsz,),
        in_specs=[pl.BlockSpec((None, rows, n), lambda b: (b, 0, 0))],
        out_specs=pl.BlockSpec((None, 3 * rows // 2, n), lambda b: (b, 0, 0)),
        out_shape=jax.ShapeDtypeStruct((bsz, 3 * rows // 2, n), F32),
        compiler_params=_params(("parallel",)),
        name="gate_scan",
    )(gt)


def _mlstm_kernel(emit_h, emit_state, *refs):
    fwd, bwd = refs[0:4], refs[4:8]
    ct0_ref, m0_ref = refs[8:10]
    outs = list(refs[10:])
    hf_ref, hb_ref = (outs.pop(0), outs.pop(0)) if emit_h else (None, None)
    cto_ref, mo_ref = (outs.pop(0), outs.pop(0)) if emit_state else (None, None)
    ct_scr, m_scr = outs
    i = pl.program_id(0)
    n_combo = 2 * HEADS
    bsz, cps, _, L = fwd[0].shape

    @pl.when(i == 0)
    def _():
        ct_scr[...] = ct0_ref[...]
        m_scr[...] = m0_ref[...]

    is_fwd_row = lax.broadcasted_iota(jnp.int32, (n_combo, L), 0) < HEADS
    is_fwd_col = lax.broadcasted_iota(jnp.int32, (n_combo, 1), 0) < HEADS
    krow = lax.broadcasted_iota(jnp.int32, (8, L), 0)
    tpos = lax.broadcasted_iota(jnp.int32, (L, L), 0)
    spos = lax.broadcasted_iota(jnp.int32, (L, L), 1)
    masks = (spos <= tpos, spos >= tpos)
    ones_rows = jnp.ones((STATE_ROWS - HEAD_DIM, L), BF16)

    def position_rows(b, slots):
        lanes = [slice(u * L, (u + 1) * L) for u in slots]
        pick = lambda r: jnp.where(is_fwd_row, fwd[3][b, r * n_combo:(r + 1) * n_combo, lanes[0]],
                                   bwd[3][b, r * n_combo:(r + 1) * n_combo, lanes[1]])
        last = lambda a: jnp.where(is_fwd_col, a[:, L - 1:L], a[:, 0:1])
        rowb, brow, cmax = pick(0), pick(1), pick(2)
        total, rowb_max = last(brow), last(cmax)
        m_in = m_scr[b][:, 0:1]
        mm = jnp.maximum(cmax, m_in)
        m_loc = total + rowb_max
        m_new = jnp.maximum(total + m_in, m_loc)
        m_scr[b] = jnp.broadcast_to(m_new, (n_combo, HEAD_DIM))
        return dict(w_inter=jnp.exp(m_in - mm), floor=jnp.exp(-(brow + mm)),
                    w_state=jnp.exp(rowb - rowb_max), a_old=jnp.exp(total + m_in - m_new),
                    a_new=jnp.exp(m_loc - m_new), rb3=_split3(rowb), nm3=_split3(-mm))

    def front(b, d, h, u, pr):
        c = d * HEADS + h
        sl = slice(h * HEAD_DIM, (h + 1) * HEAD_DIM)
        row = lambda a: a[c:c + 1, :]
        src = fwd if d == 0 else bwd
        qt, kt, vt = src[0][b, u, sl, :], src[1][b, u, sl, :], src[2][b, u, sl, :]
        vext = jnp.concatenate([vt, ones_rows], axis=0)
        ct = ct_scr[b * n_combo + c]
        out = dict(b=b, d=d, u=u, sl=sl, vext=vext, w_inter=row(pr["w_inter"]),
                   floor=row(pr["floor"]))
        if emit_h:
            nm3, rb3 = pr["nm3"], pr["rb3"]
            lhs = jnp.where(krow < 3, 1.0, jnp.where(krow == 3, row(nm3[0]), jnp.where(
                krow == 4, row(nm3[1]), jnp.where(krow == 5, row(nm3[2]), 0.0))))
            rhs = jnp.where(krow == 0, row(rb3[0]), jnp.where(krow == 1, row(rb3[1]), jnp.where(
                krow == 2, row(rb3[2]), jnp.where(krow < 6, 1.0, 0.0))))
            out["expo"] = _dot(lhs.astype(BF16), rhs.astype(BF16), _TN)
            out["s0"] = _dot(qt, kt, _TN)
            out["state_t"] = _dot(ct.astype(BF16), qt)
        ktw = (kt.astype(F32) * row(pr["w_state"])).astype(BF16)
        ckv = _dot(vext, ktw, _NT)
        ct_scr[b * n_combo + c] = pr["a_old"][c:c + 1, :] * ct + pr["a_new"][c:c + 1, :] * ckv
        return out

    def back(f):
        dmat = jnp.where(masks[f["d"]], jnp.exp(f["expo"]), 0.0)
        s = (f["s0"] * dmat).astype(BF16)
        res = f["w_inter"] * f["state_t"] + _dot(f["vext"], s, _NT)
        den = res[HEAD_DIM:HEAD_DIM + 1, :]
        h_ref = hf_ref if f["d"] == 0 else hb_ref
        h_ref[f["b"], f["u"], f["sl"], :] = res[:HEAD_DIM] / jnp.maximum(jnp.abs(den), f["floor"])

    pending = None
    for step in range(cps):
        slots = (step, cps - 1 - step)
        for b in range(bsz):
            pr = position_rows(b, slots)
            for d in range(2):
                for h in range(HEADS):
                    cur = front(b, d, h, slots[d], pr)
                    if emit_h and pending is not None:
                        back(pending)
                    pending = cur
    if emit_h:
        back(pending)

    if emit_state:
        @pl.when(i == pl.num_programs(0) - 1)
        def _():
            cto_ref[...] = ct_scr[...]
            mo_ref[...] = m_scr[...]


def _mlstm(qt, kt, vt, sc, ct0, m0, *, emit_h):
    bsz, nc, w, _ = qt.shape
    cps = MLSTM_CHUNKS_PER_STEP if nc % MLSTM_CHUNKS_PER_STEP == 0 else 1
    steps = nc // cps
    fpos = lambda i: i
    bpos = lambda i: steps - 1 - i

    def feat(pos_fn):
        return pl.BlockSpec((bsz, cps, w, CHUNK), lambda i: (0, pos_fn(i), 0, 0))

    def specs(pos_fn):
        scs = pl.BlockSpec((bsz, sc.shape[1], cps * CHUNK), lambda i: (0, 0, pos_fn(i)))
        return [feat(pos_fn)] * 3 + [scs]

    ct_spec = pl.BlockSpec(ct0.shape, lambda i: (0, 0, 0))
    m_spec = pl.BlockSpec(m0.shape, lambda i: (0, 0, 0))
    if emit_h:
        hsd = jax.ShapeDtypeStruct((bsz, nc, w, CHUNK), F32)
        out_specs = [feat(fpos), feat(bpos)]
        out_shape = [hsd, hsd]
    else:
        out_specs = [ct_spec, m_spec]
        out_shape = [jax.ShapeDtypeStruct(ct0.shape, F32), jax.ShapeDtypeStruct(m0.shape, F32)]
    return pl.pallas_call(
        functools.partial(_mlstm_kernel, emit_h, not emit_h),
        grid=(steps,),
        in_specs=specs(fpos) + specs(bpos) + [ct_spec, m_spec],
        out_specs=out_specs,
        out_shape=out_shape,
        scratch_shapes=[pltpu.VMEM(ct0.shape, F32), pltpu.VMEM(m0.shape, F32)],
        compiler_params=_params(("arbitrary",)),
        name="mlstm" if emit_h else "mlstm_ctx",
    )(qt, kt, vt, sc, qt, kt, vt, sc, ct0, m0)


def _mix_ffn_kernel(x1_ref, hf_ref, hb_ref, ot_ref, yp_ref, mod_ref, npre_ref, npost_ref, hn_ref,
                    wout_ref, w1a_ref, w1b_ref, w2_ref, out_ref, mixt_scr, y_scr, acc_scr):
    hnorm = jnp.concatenate([hn_ref[...]] * (CHUNK // HEAD_DIM), axis=1)
    for c in range(hf_ref.shape[0]):
        for h in range(HEADS):
            sl = slice(h * HEAD_DIM, (h + 1) * HEAD_DIM)
            hh = hf_ref[c, sl, :] + hb_ref[c, sl, :]
            inv = lax.rsqrt(jnp.mean(hh * hh, axis=0, keepdims=True) + EPS)
            gated = hh * inv * hnorm[sl, :] * _sigmoid(ot_ref[c, sl, :])
            mixt_scr[sl, c * CHUNK:(c + 1) * CHUNK] = gated.astype(BF16)
    z = _dot(mixt_scr[...], wout_ref[0:M_W, :], _TN) + _dot(yp_ref[...], wout_ref[M_W:, :])
    x2 = x1_ref[...] + (1.0 * mod_ref[0, 5:6, :]) * _rms(z, npost_ref[1:2, :])
    y = _rms(x2, npre_ref[2:3, :]) * (1.0 + mod_ref[0, 7:8, :]) + mod_ref[0, 6:7, :]
    y_scr[...] = y.astype(BF16)
    _swiglu_into(acc_scr, y_scr, w1a_ref, w1b_ref, w2_ref)
    out_ref[...] = x2 + (0.5 * mod_ref[0, 8:9, :]) * _rms(acc_scr[...], npost_ref[2:3, :])


def _mix_ffn(x1, hf, hb, ot, yp, mod_all, npre, npost, hnorm, wout, w1a, w1b, w2, *, tm):
    bsz, n, d = x1.shape
    tok = lambda w: pl.BlockSpec((None, tm, w), lambda b, i: (b, i, 0))
    feat = pl.BlockSpec((None, tm // CHUNK, M_W, CHUNK), lambda b, i: (b, i, 0, 0))
    in_specs = [tok(d), feat, feat, feat, tok(POOL_W),
                pl.BlockSpec((1, N_MOD, d), lambda b, i: (b, 0, 0)),
                _const_spec(npre.shape), _const_spec(npost.shape), _const_spec(hnorm.shape),
                _const_spec(wout.shape), _const_spec(w1a.shape), _const_spec(w1b.shape),
                _const_spec(w2.shape)]
    return pl.pallas_call(
        _mix_ffn_kernel,
        grid=(bsz, n // tm),
        in_specs=in_specs,
        out_specs=tok(d),
        out_shape=jax.ShapeDtypeStruct((bsz, n, d), F32),
        scratch_shapes=[pltpu.VMEM((M_W, tm), BF16), pltpu.VMEM((tm, d), BF16),
                        pltpu.VMEM((tm, d), F32)],
        compiler_params=_params(("parallel", "arbitrary")),
        name="mix_ffn",
    )(x1, hf, hb, ot, yp, mod_all, npre, npost, hnorm, wout, w1a, w1b, w2)


def _ffn_weights(w_in, w_out):
    ff = w_in.shape[1] // 2
    return w_in[:, :ff].astype(BF16), w_in[:, ff:].astype(BF16), w_out.astype(BF16)


def kernel(x, c, ctx, c_ctx, w_mod, b_mod, norm_pre, norm_post, ffn_w_in, ffn_w_out, w_in, w_out,
           conv_w, conv_b, w_q, w_k, i_bias, f_bias, head_norm, pool_w, pool_scale):
    assert w_mod.shape[0] == 1, "single-layer stack"
    bsz, n, d = x.shape
    n_ctx = ctx.shape[1]
    m_cols = 3 * M_W + GATE_COLS

    rows = 8
    s_rows = jnp.zeros((rows, d), F32).at[:bsz].set(c).at[bsz].set(c_ctx)
    mod_all = _modulation(s_rows, w_mod[0], b_mod[0]).reshape(rows, N_MOD, d)

    w1a, w1b, w2 = _ffn_weights(ffn_w_in[0, 0], ffn_w_out[0, 0])
    v1a, v1b, v2 = _ffn_weights(ffn_w_in[0, 1], ffn_w_out[0, 1])
    wi = w_in[0]
    gate_w = jnp.pad(wi[:, 3 * M_W:m_cols], ((0, 0), (0, GATE_PAD - GATE_COLS)))
    win_ctx = jnp.concatenate([wi[:, :M_W], gate_w], axis=1).astype(BF16)
    win_lat = jnp.concatenate([win_ctx, wi[:, m_cols:].astype(BF16)], axis=1)
    wint_lat = wi[:, M_W:3 * M_W].T.astype(BF16)
    wint_ctx = wint_lat[:M_W]
    gbias = jnp.pad(jnp.concatenate([i_bias[0].reshape(-1), f_bias[0].reshape(-1)]),
                    (0, GATE_PAD - GATE_COLS)).reshape(1, GATE_PAD)
    npre, npost = norm_pre[0], norm_post[0]

    tm = min(512, n)
    x1, qk_l, g_l, vt_l, pool_l, ot_l = _ffn_in(x, mod_all, None, npre, npost, w1a, w1b, w2,
                                                win_lat, wint_lat, gbias, with_rest=True, tm=tm)
    qk_c, g_c, vt_c = _ffn_in(ctx, mod_all, bsz, npre, npost, w1a, w1b, w2, win_ctx, wint_ctx,
                              gbias, with_rest=False, tm=min(512, n_ctx))

    cb = conv_b[0].reshape(1, M_W)
    wqt = jnp.swapaxes(w_q[0], 1, 2).astype(BF16)
    wkt = jnp.swapaxes(w_k[0], 1, 2).astype(BF16)
    qt_l, kt_l = _qk_proj(qk_l, conv_w[0], cb, wqt, wkt, tq=min(1024, n))
    qt_c, kt_c = _qk_proj(qk_c, conv_w[0], cb, wqt, wkt, tq=min(1024, n_ctx))
    gt_l = jnp.swapaxes(g_l[:, :, :GATE_COLS], 1, 2)
    gt_c = jnp.swapaxes(g_c[:, :, :GATE_COLS], 1, 2)

    y_pool = _pool_mixer(pool_l, pool_w[0].astype(BF16), pool_scale[0].reshape(1, POOL_W),
                         tt=min(2048, n))

    ct0 = jnp.zeros((bsz * 2 * HEADS, STATE_ROWS, HEAD_DIM), F32)
    m0 = jnp.zeros((bsz, 2 * HEADS, HEAD_DIM), F32)
    ct1, m1 = _mlstm(qt_c, kt_c, vt_c, _gate_scan(gt_c), ct0, m0, emit_h=False)
    ht_f, ht_b = _mlstm(qt_l, kt_l, vt_l, _gate_scan(gt_l), ct1, m1, emit_h=True)

    hnorm = jnp.broadcast_to(head_norm[0].reshape(M_W, 1), (M_W, HEAD_DIM))
    return _mix_ffn(x1, ht_f, ht_b, ot_l, y_pool, mod_all, npre, npost, hnorm,
                    w_out[0].astype(BF16), v1a, v1b, v2, tm=tm)
```

```python
import functools

import jax
import jax.numpy as jnp
from jax import lax
from jax.experimental import pallas as pl
from jax.experimental.pallas import tpu as pltpu

F32 = jnp.float32
BF16 = jnp.bfloat16

EPS = 1e-6
N_MOD = 9
HEADS = 4
HEAD_DIM = 128
M_W = HEADS * HEAD_DIM
POOL_WINDOWS = (2, 4, 8, 16)
POOL_GW = 128
POOL_W = POOL_GW * len(POOL_WINDOWS)
GRID_W = 64
POOL_PAD = 8
CHUNK = 256
GATE_COLS = 4 * HEADS
GATE_PAD = 128
FF_CHUNK = 256
STATE_ROWS = HEAD_DIM + 16
MLSTM_CHUNKS_PER_STEP = 2
V7X_VMEM_LIMIT = 56 * 1024 * 1024

_NT = (((1,), (1,)), ((), ()))
_TN = (((0,), (0,)), ((), ()))


def _sigmoid(x):
    return 1.0 / (1.0 + jnp.exp(-x))


def _log_sigmoid(x):
    return jnp.minimum(x, 0.0) - jnp.log(1.0 + jnp.exp(-jnp.abs(x)))


def _rms(x, g):
    ms = jnp.mean(x * x, axis=-1, keepdims=True)
    return x * lax.rsqrt(ms + EPS) * g


def _dot(a, b, dims=None):
    if dims is None:
        return jnp.dot(a, b, preferred_element_type=F32)
    return lax.dot_general(a, b, dims, preferred_element_type=F32)


def _const_spec(shape):
    nd = len(shape)
    return pl.BlockSpec(shape, lambda *_: (0,) * nd, pipeline_mode=pl.Buffered(1))


def _params(sem):
    return pltpu.CompilerParams(dimension_semantics=sem, vmem_limit_bytes=V7X_VMEM_LIMIT)


def _mod_kernel(s_ref, w_ref, b_ref, o_ref):
    s = s_ref[...]
    act = (s * _sigmoid(s)).astype(BF16)
    o_ref[...] = _dot(act, w_ref[...].astype(BF16)) + b_ref[...]


def _modulation(s_rows, w_mod, b_mod):
    rows, d = s_rows.shape
    n = w_mod.shape[1]
    tn = 1024
    return pl.pallas_call(
        _mod_kernel,
        grid=(n // tn,),
        in_specs=[pl.BlockSpec((rows, d), lambda j: (0, 0)),
                  pl.BlockSpec((d, tn), lambda j: (0, j)),
                  pl.BlockSpec((1, tn), lambda j: (0, j))],
        out_specs=pl.BlockSpec((rows, tn), lambda j: (0, j)),
        out_shape=jax.ShapeDtypeStruct((rows, n), F32),
        compiler_params=_params(("arbitrary",)),
        name="mod",
    )(s_rows, w_mod, b_mod.reshape(1, n))


def _prenorm_operand(x, gain, scale, y_ref, r_ref):
    r = lax.rsqrt(jnp.mean(x * x, axis=-1, keepdims=True) + EPS)
    r_ref[...] = jnp.broadcast_to(r, r_ref.shape)
    y_ref[...] = (x * (gain * (1.0 + scale))).astype(BF16)


def _lanes(r_ref, width):
    return jnp.concatenate([r_ref[...]] * (width // r_ref.shape[1]), axis=1)


def _shift_rows(shift, rows):
    return jnp.broadcast_to(shift, (rows, shift.shape[1])).astype(BF16)


def _swiglu_into(acc_ref, y_ref, r_ref, shw_ref, w1a_ref, w1b_ref, w2_ref):
    y = y_ref[...]
    ff = w2_ref.shape[0]
    n_chunks = ff // FF_CHUNK
    rr = _lanes(r_ref, FF_CHUNK)
    cols = lambda j, base=0: slice(base + j * FF_CHUNK, base + (j + 1) * FF_CHUNK)
    up = lambda j: (_dot(y, w1a_ref[:, cols(j)]), _dot(y, w1b_ref[:, cols(j)]))
    nxt = up(0)
    for j in range(n_chunks):
        a, b = nxt
        if j + 1 < n_chunks:
            nxt = up(j + 1)
        a = a * rr + shw_ref[0:1, cols(j)]
        b = b * rr + shw_ref[0:1, cols(j, ff)]
        h = ((a * _sigmoid(a)) * b).astype(BF16)
        part = _dot(h, w2_ref[cols(j), :])
        if j == 0:
            acc_ref[...] = part
        else:
            acc_ref[...] += part


def _swiglu_shift(shw_ref, shift, w1a_ref, w1b_ref):
    ff = w1a_ref.shape[1]
    sh = _shift_rows(shift, shw_ref.shape[0])
    shw_ref[:, 0:ff] = _dot(sh, w1a_ref[...])
    shw_ref[:, ff:2 * ff] = _dot(sh, w1b_ref[...])


def _ffn_in_kernel(with_rest, x_ref, mod_ref, npre_ref, npost_ref, w1a_ref, w1b_ref, w2_ref,
                   win_ref, wint_ref, gbias_ref, *rest):
    if with_rest:
        x1_ref, qk_ref, g_ref, vt_ref, pool_ref, ot_ref = rest[:6]
        rest = rest[6:]
    else:
        qk_ref, g_ref, vt_ref = rest[:3]
        rest = rest[3:]
    y_scr, acc_scr, r_scr, shw1_scr, shwin_scr, shwt_scr = rest
    tm = x_ref.shape[0]

    @pl.when(pl.program_id(1) == 0)
    def _():
        _swiglu_shift(shw1_scr, mod_ref[0, 0:1, :], w1a_ref, w1b_ref)
        shwin_scr[...] = _dot(_shift_rows(mod_ref[0, 3:4, :], shwin_scr.shape[0]), win_ref[...])
        shwt_scr[...] = _dot(wint_ref[...], _shift_rows(mod_ref[0, 3:4, :], shwt_scr.shape[1]), _NT)

    x = x_ref[...]
    _prenorm_operand(x, npre_ref[0:1, :], mod_ref[0, 1:2, :], y_scr, r_scr)
    _swiglu_into(acc_scr, y_scr, r_scr, shw1_scr, w1a_ref, w1b_ref, w2_ref)
    x1 = x + (0.5 * mod_ref[0, 2:3, :]) * _rms(acc_scr[...], npost_ref[0:1, :])
    _prenorm_operand(x1, npre_ref[1:2, :], mod_ref[0, 4:5, :], y_scr, r_scr)
    y2b = y_scr[...]
    c0, c1 = M_W, M_W + GATE_PAD
    wide = _lanes(r_scr, M_W)
    r_row = r_scr[...].T[0:1, :]
    shwt = _lanes(shwt_scr, tm)
    qk_ref[...] = _dot(y2b, win_ref[:, 0:c0]) * wide + shwin_scr[0:1, 0:c0]
    g_ref[...] = (_dot(y2b, win_ref[:, c0:c1]) * r_scr[...] + shwin_scr[0:1, c0:c1]
                  + gbias_ref[...])
    vt = (_dot(wint_ref[0:M_W, :], y2b, _NT) * r_row + shwt[0:M_W, :]).astype(BF16)
    for c in range(vt_ref.shape[0]):
        vt_ref[c] = vt[:, c * CHUNK:(c + 1) * CHUNK]
    if with_rest:
        x1_ref[...] = x1
        pool_ref[...] = _dot(y2b, win_ref[:, c1:c1 + POOL_W]) * wide + shwin_scr[0:1, c1:c1 + POOL_W]
        ot = _dot(wint_ref[M_W:2 * M_W, :], y2b, _NT) * r_row + shwt[M_W:2 * M_W, :]
        for c in range(ot_ref.shape[0]):
            ot_ref[c] = ot[:, c * CHUNK:(c + 1) * CHUNK]


def _ffn_in(x, mod_all, mod_row0, npre, npost, w1a, w1b, w2, win, wint, gbias, *, with_rest, tm):
    bsz, n, d = x.shape
    grid = (bsz, n // tm)
    tok = lambda w: pl.BlockSpec((None, tm, w), lambda b, i: (b, i, 0))
    feat = pl.BlockSpec((None, tm // CHUNK, M_W, CHUNK), lambda b, i: (b, i, 0, 0))
    if mod_row0 is None:
        mod_spec = pl.BlockSpec((1, N_MOD, d), lambda b, i: (b, 0, 0))
    else:
        mod_spec = pl.BlockSpec((1, N_MOD, d), lambda b, i: (mod_row0, 0, 0))
    in_specs = [tok(d), mod_spec, _const_spec(npre.shape), _const_spec(npost.shape),
                _const_spec(w1a.shape), _const_spec(w1b.shape), _const_spec(w2.shape),
                _const_spec(win.shape), _const_spec(wint.shape), _const_spec(gbias.shape)]
    sds = lambda w, dt: jax.ShapeDtypeStruct((bsz, n, w), dt)
    fsds = lambda dt: jax.ShapeDtypeStruct((bsz, n // CHUNK, M_W, CHUNK), dt)
    out_specs = [tok(M_W), tok(GATE_PAD), feat]
    out_shape = [sds(M_W, F32), sds(GATE_PAD, F32), fsds(BF16)]
    if with_rest:
        out_specs = [tok(d)] + out_specs + [tok(POOL_W), feat]
        out_shape = [sds(d, F32)] + out_shape + [sds(POOL_W, F32), fsds(F32)]
    return pl.pallas_call(
        functools.partial(_ffn_in_kernel, with_rest),
        grid=grid,
        in_specs=in_specs,
        out_specs=out_specs,
        out_shape=out_shape,
        scratch_shapes=[pltpu.VMEM((tm, d), BF16), pltpu.VMEM((tm, d), F32),
                        pltpu.VMEM((tm, 128), F32), pltpu.VMEM((8, 2 * w1a.shape[1]), F32),
                        pltpu.VMEM((8, win.shape[1]), F32), pltpu.VMEM((wint.shape[0], 128), F32)],
        compiler_params=_params(("parallel", "arbitrary")),
        name="ffn_in" if with_rest else "ffn_in_ctx",
    )(x, mod_all, npre, npost, w1a, w1b, w2, win, wint, gbias)


def _qk_kernel(nt, x_ref, prev_ref, next_ref, cw_ref, cb_ref, wqt_ref, wkt_ref, qt_ref, kt_ref):
    j = pl.program_id(1)
    x = x_ref[...]
    tq = x.shape[0]
    prev_row = prev_ref[7:8, :] * jnp.where(j > 0, 1.0, 0.0)
    next_row = next_ref[0:1, :] * jnp.where(j < nt - 1, 1.0, 0.0)
    row = lax.broadcasted_iota(jnp.int32, x.shape, 0)
    up = jnp.where(row == 0, prev_row, pltpu.roll(x, 1, 0))
    dn = jnp.where(row == tq - 1, next_row, pltpu.roll(x, tq - 1, 0))
    conv = cb_ref[...] + up * cw_ref[0:1, :] + x * cw_ref[1:2, :] + dn * cw_ref[2:3, :]
    u = (conv * _sigmoid(conv)).astype(BF16)
    scale = HEAD_DIM ** -0.5
    for h in range(HEADS):
        sl = slice(h * HEAD_DIM, (h + 1) * HEAD_DIM)
        uh = u[:, sl]
        qt = _dot(wqt_ref[h], uh, _NT).astype(BF16)
        kt = (_dot(wkt_ref[h], uh, _NT) * scale).astype(BF16)
        for c in range(qt_ref.shape[0]):
            qt_ref[c, sl, :] = qt[:, c * CHUNK:(c + 1) * CHUNK]
            kt_ref[c, sl, :] = kt[:, c * CHUNK:(c + 1) * CHUNK]


def _qk_proj(qk_src, conv_w, conv_b, wqt, wkt, *, tq):
    bsz, n, w = qk_src.shape
    nt = n // tq
    blk8 = tq // 8
    last8 = n // 8 - 1
    tok = pl.BlockSpec((None, tq, w), lambda b, j: (b, j, 0))
    prev = pl.BlockSpec((None, 8, w), lambda b, j: (b, jnp.maximum(j * blk8 - 1, 0), 0))
    nxt = pl.BlockSpec((None, 8, w), lambda b, j: (b, jnp.minimum((j + 1) * blk8, last8), 0))
    feat = pl.BlockSpec((None, tq // CHUNK, w, CHUNK), lambda b, j: (b, j, 0, 0))
    out = jax.ShapeDtypeStruct((bsz, n // CHUNK, w, CHUNK), BF16)
    return pl.pallas_call(
        functools.partial(_qk_kernel, nt),
        grid=(bsz, nt),
        in_specs=[tok, prev, nxt, _const_spec(conv_w.shape), _const_spec(conv_b.shape),
                  _const_spec(wqt.shape), _const_spec(wkt.shape)],
        out_specs=[feat, feat],
        out_shape=[out, out],
        compiler_params=_params(("parallel", "arbitrary")),
        name="qk",
    )(qk_src, qk_src, qk_src, conv_w, conv_b, wqt, wkt)


def _pool_kernel(nt, grid_h, x_ref, prev_ref, next_ref, wp_ref, ps_ref, y_ref, xs_ref, *bufs):
    j = pl.program_id(1)
    tt = x_ref.shape[0]
    halo = prev_ref.shape[0]
    halo_rows = halo // GRID_W
    rows = tt // GRID_W
    d0, d1 = POOL_PAD, POOL_PAD + GRID_W
    xs_ref[0:halo, :] = prev_ref[...] * jnp.where(j > 0, 1.0, 0.0)
    xs_ref[halo:halo + tt, :] = x_ref[...]
    xs_ref[halo + tt:halo + tt + halo, :] = next_ref[...] * jnp.where(j < nt - 1, 1.0, 0.0)
    zpad = jnp.zeros((rows, POOL_PAD, POOL_GW), F32)
    for buf in bufs:
        buf[:, 0:d0, :] = zpad
        buf[:, d1:d1 + POOL_PAD, :] = zpad
    c = lax.broadcasted_iota(jnp.int32, (1, GRID_W, POOL_GW), 1)
    r = j * rows + lax.broadcasted_iota(jnp.int32, (rows, 1, POOL_GW), 0)

    def shifted_sum(src, dst, k):
        dst[:, d0:d1, :] = src[:, d0:d1, :] + src[:, d0 + k:d1 + k, :]

    for g, win in enumerate(POOL_WINDOWS):
        lo, hi = win // 2, win - 1 - win // 2
        sl = slice(g * POOL_GW, (g + 1) * POOL_GW)
        first = (halo_rows - lo) * GRID_W
        vs = xs_ref[first:first + (rows + win - 1) * GRID_W, sl]
        span = 1
        while span < win:
            vs = vs[:-span * GRID_W] + vs[span * GRID_W:]
            span *= 2
        a, b, cc, dd = bufs
        a[:, d0:d1, :] = vs.reshape(rows, GRID_W, POOL_GW)
        trail, lead, spare_t, spare_l, k = a, a, b, dd, 1
        while k < lo:
            shifted_sum(trail, spare_t, -k)
            shifted_sum(lead, spare_l, k)
            trail, spare_t = spare_t, (cc if spare_t is b else b)
            lead, spare_l = spare_l, (a if spare_l is dd else dd)
            k *= 2
        tot = trail[:, d0 - 1:d1 - 1, :] + lead[:, d0:d1, :]
        cnt_c = jnp.minimum(c + hi, GRID_W - 1) - jnp.maximum(c - lo, 0) + 1
        cnt_r = jnp.minimum(r + hi, grid_h - 1) - jnp.maximum(r - lo, 0) + 1
        pg = tot * (1.0 / cnt_c.astype(F32)) * (1.0 / cnt_r.astype(F32))
        dlt = (pg.reshape(tt, POOL_GW) - x_ref[:, sl]).astype(BF16)
        y_ref[:, sl] = (_dot(dlt, wp_ref[g]) * ps_ref[:, sl]).astype(BF16)


def _pool_mixer(u, wp, ps, *, tt):
    bsz, n, w = u.shape
    nt = n // tt
    halo = 8 * GRID_W
    hb = tt // halo
    last = n // halo - 1
    tok = pl.BlockSpec((None, tt, w), lambda b, j: (b, j, 0))
    prev = pl.BlockSpec((None, halo, w), lambda b, j: (b, jnp.maximum(j * hb - 1, 0), 0))
    nxt = pl.BlockSpec((None, halo, w), lambda b, j: (b, jnp.minimum((j + 1) * hb, last), 0))
    return pl.pallas_call(
        functools.partial(_pool_kernel, nt, n // GRID_W),
        grid=(bsz, nt),
        in_specs=[tok, prev, nxt, _const_spec(wp.shape), _const_spec(ps.shape)],
        out_specs=tok,
        out_shape=jax.ShapeDtypeStruct((bsz, n, w), BF16),
        scratch_shapes=[pltpu.VMEM((tt + 2 * halo, w), F32)]
        + [pltpu.VMEM((tt // GRID_W, GRID_W + 2 * POOL_PAD, POOL_GW), F32)] * 4,
        compiler_params=_params(("parallel", "arbitrary")),
        name="pool",
    )(u, u, u, wp, ps)


def _split3(x):
    hi = x.astype(BF16).astype(F32)
    r = x - hi
    mid = r.astype(BF16).astype(F32)
    lo = (r - mid).astype(BF16).astype(F32)
    return hi, mid, lo


def _gate_scan_kernel(chunk, gt_ref, o_ref):
    n_combo = 2 * HEADS
    g = gt_ref[...]
    n = g.shape[1]
    pos = jnp.bitwise_and(lax.broadcasted_iota(jnp.int32, (n_combo, n), 1), chunk - 1)
    is_fwd_row = lax.broadcasted_iota(jnp.int32, (n_combo, n), 0) < HEADS

    def scan(x, op, ident):
        sft = 1
        while sft < chunk:
            from_l = jnp.where(pos >= sft, pltpu.roll(x, sft, 1), ident)
            from_r = jnp.where(pos < chunk - sft, pltpu.roll(x, n - sft, 1), ident)
            x = op(x, jnp.where(is_fwd_row, from_l, from_r))
            sft *= 2
        return x

    brow = scan(_log_sigmoid(g[n_combo:]), jnp.add, 0.0)
    rowb = g[:n_combo] - brow
    o_ref[0:n_combo, :] = rowb
    o_ref[n_combo:2 * n_combo, :] = brow
    o_ref[2 * n_combo:, :] = scan(rowb, jnp.maximum, -jnp.inf)


def _gate_scan(gt):
    bsz, rows, n = gt.shape
    return pl.pallas_call(
        functools.partial(_gate_scan_kernel, CHUNK),
        grid=(bsz,),
        in_specs=[pl.BlockSpec((None, rows, n), lambda b: (b, 0, 0))],
        out_specs=pl.BlockSpec((None, 3 * rows // 2, n), lambda b: (b, 0, 0)),
        out_shape=jax.ShapeDtypeStruct((bsz, 3 * rows // 2, n), F32),
        compiler_params=_params(("parallel",)),
        name="gate_scan",
    )(gt)


def _mlstm_kernel(emit_h, emit_state, *refs):
    fwd, bwd = refs[0:4], refs[4:8]
    ct0_ref, m0_ref = refs[8:10]
    outs = list(refs[10:])
    hf_ref, hb_ref = (outs.pop(0), outs.pop(0)) if emit_h else (None, None)
    cto_ref, mo_ref = (outs.pop(0), outs.pop(0)) if emit_state else (None, None)
    ct_scr, m_scr = outs
    i = pl.program_id(0)
    n_combo = 2 * HEADS
    bsz, cps, _, L = fwd[0].shape

    @pl.when(i == 0)
    def _():
        ct_scr[...] = ct0_ref[...]
        m_scr[...] = m0_ref[...]

    is_fwd_row = lax.broadcasted_iota(jnp.int32, (n_combo, L), 0) < HEADS
    is_fwd_col = lax.broadcasted_iota(jnp.int32, (n_combo, 1), 0) < HEADS
    krow = lax.broadcasted_iota(jnp.int32, (8, L), 0)
    tpos = lax.broadcasted_iota(jnp.int32, (L, L), 0)
    spos = lax.broadcasted_iota(jnp.int32, (L, L), 1)
    masks = (spos <= tpos, spos >= tpos)
    ones_rows = jnp.ones((STATE_ROWS - HEAD_DIM, L), BF16)

    def position_rows(b, slots):
        lanes = [slice(u * L, (u + 1) * L) for u in slots]
        pick = lambda r: jnp.where(is_fwd_row, fwd[3][b, r * n_combo:(r + 1) * n_combo, lanes[0]],
                                   bwd[3][b, r * n_combo:(r + 1) * n_combo, lanes[1]])
        last = lambda a: jnp.where(is_fwd_col, a[:, L - 1:L], a[:, 0:1])
        rowb, brow, cmax = pick(0), pick(1), pick(2)
        total, rowb_max = last(brow), last(cmax)
        m_in = m_scr[b][:, 0:1]
        mm = jnp.maximum(cmax, m_in)
        m_loc = total + rowb_max
        m_new = jnp.maximum(total + m_in, m_loc)
        m_scr[b] = jnp.broadcast_to(m_new, (n_combo, HEAD_DIM))
        return dict(w_inter=jnp.exp(m_in - mm), floor=jnp.exp(-(brow + mm)),
                    w_state=jnp.exp(rowb - rowb_max), a_old=jnp.exp(total + m_in - m_new),
                    a_new=jnp.exp(m_loc - m_new), rb3=_split3(rowb), nm3=_split3(-mm))

    def front(b, d, h, u, pr):
        c = d * HEADS + h
        sl = slice(h * HEAD_DIM, (h + 1) * HEAD_DIM)
        row = lambda a: a[c:c + 1, :]
        src = fwd if d == 0 else bwd
        qt, kt, vt = src[0][b, u, sl, :], src[1][b, u, sl, :], src[2][b, u, sl, :]
        vext = jnp.concatenate([vt, ones_rows], axis=0)
        ct = ct_scr[b * n_combo + c]
        out = dict(b=b, d=d, u=u, sl=sl, vext=vext, w_inter=row(pr["w_inter"]),
                   floor=row(pr["floor"]))
        if emit_h:
            nm3, rb3 = pr["nm3"], pr["rb3"]
            lhs = jnp.where(krow < 3, 1.0, jnp.where(krow == 3, row(nm3[0]), jnp.where(
                krow == 4, row(nm3[1]), jnp.where(krow == 5, row(nm3[2]), 0.0))))
            rhs = jnp.where(krow == 0, row(rb3[0]), jnp.where(krow == 1, row(rb3[1]), jnp.where(
                krow == 2, row(rb3[2]), jnp.where(krow < 6, 1.0, 0.0))))
            out["expo"] = _dot(lhs.astype(BF16), rhs.astype(BF16), _TN)
            out["s0"] = _dot(qt, kt, _TN)
            out["state_t"] = _dot(ct.astype(BF16), qt)
        ktw = (kt.astype(F32) * row(pr["w_state"])).astype(BF16)
        ckv = _dot(vext, ktw, _NT)
        ct_scr[b * n_combo + c] = pr["a_old"][c:c + 1, :] * ct + pr["a_new"][c:c + 1, :] * ckv
        return out

    def back(f):
        dmat = jnp.where(masks[f["d"]], jnp.exp(f["expo"]), 0.0)
        s = (f["s0"] * dmat).astype(BF16)
        res = f["w_inter"] * f["state_t"] + _dot(f["vext"], s, _NT)
        den = res[HEAD_DIM:HEAD_DIM + 1, :]
        h_ref = hf_ref if f["d"] == 0 else hb_ref
        h_ref[f["b"], f["u"], f["sl"], :] = res[:HEAD_DIM] / jnp.maximum(jnp.abs(den), f["floor"])

    pending = None
    for step in range(cps):
        slots = (step, cps - 1 - step)
        for b in range(bsz):
            pr = position_rows(b, slots)
            for d in range(2):
                for h in range(HEADS):
                    cur = front(b, d, h, slots[d], pr)
                    if emit_h and pending is not None:
                        back(pending)
                    pending = cur
    if emit_h:
        back(pending)

    if emit_state:
        @pl.when(i == pl.num_programs(0) - 1)
        def _():
            cto_ref[...] = ct_scr[...]
            mo_ref[...] = m_scr[...]


def _mlstm(qt, kt, vt, sc, ct0, m0, *, emit_h):
    bsz, nc, w, _ = qt.shape
    cps = MLSTM_CHUNKS_PER_STEP if nc % MLSTM_CHUNKS_PER_STEP == 0 else 1
    steps = nc // cps
    fpos = lambda i: i
    bpos = lambda i: steps - 1 - i

    def feat(pos_fn):
        return pl.BlockSpec((bsz, cps, w, CHUNK), lambda i: (0, pos_fn(i), 0, 0))

    def specs(pos_fn):
        scs = pl.BlockSpec((bsz, sc.shape[1], cps * CHUNK), lambda i: (0, 0, pos_fn(i)))
        return [feat(pos_fn)] * 3 + [scs]

    ct_spec = pl.BlockSpec(ct0.shape, lambda i: (0, 0, 0))
    m_spec = pl.BlockSpec(m0.shape, lambda i: (0, 0, 0))
    if emit_h:
        hsd = jax.ShapeDtypeStruct((bsz, nc, w, CHUNK), F32)
        out_specs = [feat(fpos), feat(bpos)]
        out_shape = [hsd, hsd]
    else:
        out_specs = [ct_spec, m_spec]
        out_shape = [jax.ShapeDtypeStruct(ct0.shape, F32), jax.ShapeDtypeStruct(m0.shape, F32)]
    return pl.pallas_call(
        functools.partial(_mlstm_kernel, emit_h, not emit_h),
        grid=(steps,),
        in_specs=specs(fpos) + specs(bpos) + [ct_spec, m_spec],
        out_specs=out_specs,
        out_shape=out_shape,
        scratch_shapes=[pltpu.VMEM(ct0.shape, F32), pltpu.VMEM(m0.shape, F32)],
        compiler_params=_params(("arbitrary",)),
        name="mlstm" if emit_h else "mlstm_ctx",
    )(qt, kt, vt, sc, qt, kt, vt, sc, ct0, m0)


def _mix_ffn_kernel(x1_ref, hf_ref, hb_ref, ot_ref, yp_ref, mod_ref, npre_ref, npost_ref, hn_ref,
                    wout_ref, w1a_ref, w1b_ref, w2_ref, out_ref, mixt_scr, y_scr, acc_scr, r_scr,
                    shw_scr):
    @pl.when(pl.program_id(1) == 0)
    def _():
        _swiglu_shift(shw_scr, mod_ref[0, 6:7, :], w1a_ref, w1b_ref)

    hnorm = jnp.concatenate([hn_ref[...]] * (CHUNK // HEAD_DIM), axis=1)
    for c in range(hf_ref.shape[0]):
        for h in range(HEADS):
            sl = slice(h * HEAD_DIM, (h + 1) * HEAD_DIM)
            hh = hf_ref[c, sl, :] + hb_ref[c, sl, :]
            inv = lax.rsqrt(jnp.mean(hh * hh, axis=0, keepdims=True) + EPS)
            gated = hh * inv * hnorm[sl, :] * _sigmoid(ot_ref[c, sl, :])
            mixt_scr[sl, c * CHUNK:(c + 1) * CHUNK] = gated.astype(BF16)
    z = _dot(mixt_scr[...], wout_ref[0:M_W, :], _TN) + _dot(yp_ref[...], wout_ref[M_W:, :])
    x2 = x1_ref[...] + (1.0 * mod_ref[0, 5:6, :]) * _rms(z, npost_ref[1:2, :])
    _prenorm_operand(x2, npre_ref[2:3, :], mod_ref[0, 7:8, :], y_scr, r_scr)
    _swiglu_into(acc_scr, y_scr, r_scr, shw_scr, w1a_ref, w1b_ref, w2_ref)
    out_ref[...] = x2 + (0.5 * mod_ref[0, 8:9, :]) * _rms(acc_scr[...], npost_ref[2:3, :])


def _mix_ffn(x1, hf, hb, ot, yp, mod_all, npre, npost, hnorm, wout, w1a, w1b, w2, *, tm):
    bsz, n, d = x1.shape
    tok = lambda w: pl.BlockSpec((None, tm, w), lambda b, i: (b, i, 0))
    feat = pl.BlockSpec((None, tm // CHUNK, M_W, CHUNK), lambda b, i: (b, i, 0, 0))
    in_specs = [tok(d), feat, feat, feat, tok(POOL_W),
                pl.BlockSpec((1, N_MOD, d), lambda b, i: (b, 0, 0)),
                _const_spec(npre.shape), _const_spec(npost.shape), _const_spec(hnorm.shape),
                _const_spec(wout.shape), _const_spec(w1a.shape), _const_spec(w1b.shape),
                _const_spec(w2.shape)]
    return pl.pallas_call(
        _mix_ffn_kernel,
        grid=(bsz, n // tm),
        in_specs=in_specs,
        out_specs=tok(d),
        out_shape=jax.ShapeDtypeStruct((bsz, n, d), F32),
        scratch_shapes=[pltpu.VMEM((M_W, tm), BF16), pltpu.VMEM((tm, d), BF16),
                        pltpu.VMEM((tm, d), F32), pltpu.VMEM((tm, 128), F32),
                        pltpu.VMEM((8, 2 * w1a.shape[1]), F32)],
        compiler_params=_params(("parallel", "arbitrary")),
        name="mix_ffn",
    )(x1, hf, hb, ot, yp, mod_all, npre, npost, hnorm, wout, w1a, w1b, w2)


def _ffn_weights(w_in, w_out):
    ff = w_in.shape[1] // 2
    return w_in[:, :ff].astype(BF16), w_in[:, ff:].astype(BF16), w_out.astype(BF16)


def kernel(x, c, ctx, c_ctx, w_mod, b_mod, norm_pre, norm_post, ffn_w_in, ffn_w_out, w_in, w_out,
           conv_w, conv_b, w_q, w_k, i_bias, f_bias, head_norm, pool_w, pool_scale):
    assert w_mod.shape[0] == 1, "single-layer stack"
    bsz, n, d = x.shape
    n_ctx = ctx.shape[1]
    m_cols = 3 * M_W + GATE_COLS

    rows = 8
    s_rows = jnp.zeros((rows, d), F32).at[:bsz].set(c).at[bsz].set(c_ctx)
    mod_all = _modulation(s_rows, w_mod[0], b_mod[0]).reshape(rows, N_MOD, d)

    w1a, w1b, w2 = _ffn_weights(ffn_w_in[0, 0], ffn_w_out[0, 0])
    v1a, v1b, v2 = _ffn_weights(ffn_w_in[0, 1], ffn_w_out[0, 1])
    wi = w_in[0]
    gate_w = jnp.pad(wi[:, 3 * M_W:m_cols], ((0, 0), (0, GATE_PAD - GATE_COLS)))
    win_ctx = jnp.concatenate([wi[:, :M_W], gate_w], axis=1).astype(BF16)
    win_lat = jnp.concatenate([win_ctx, wi[:, m_cols:].astype(BF16)], axis=1)
    wint_lat = wi[:, M_W:3 * M_W].T.astype(BF16)
    wint_ctx = wint_lat[:M_W]
    gbias = jnp.pad(jnp.concatenate([i_bias[0].reshape(-1), f_bias[0].reshape(-1)]),
                    (0, GATE_PAD - GATE_COLS)).reshape(1, GATE_PAD)
    npre, npost = norm_pre[0], norm_post[0]

    tm = min(512, n)
    x1, qk_l, g_l, vt_l, pool_l, ot_l = _ffn_in(x, mod_all, None, npre, npost, w1a, w1b, w2,
                                                win_lat, wint_lat, gbias, with_rest=True, tm=tm)
    qk_c, g_c, vt_c = _ffn_in(ctx, mod_all, bsz, npre, npost, w1a, w1b, w2, win_ctx, wint_ctx,
                              gbias, with_rest=False, tm=min(512, n_ctx))

    cb = conv_b[0].reshape(1, M_W)
    wqt = jnp.swapaxes(w_q[0], 1, 2).astype(BF16)
    wkt = jnp.swapaxes(w_k[0], 1, 2).astype(BF16)
    qt_l, kt_l = _qk_proj(qk_l, conv_w[0], cb, wqt, wkt, tq=min(1024, n))
    qt_c, kt_c = _qk_proj(qk_c, conv_w[0], cb, wqt, wkt, tq=min(1024, n_ctx))
    gt_l = jnp.swapaxes(g_l[:, :, :GATE_COLS], 1, 2)
    gt_c = jnp.swapaxes(g_c[:, :, :GATE_COLS], 1, 2)

    y_pool = _pool_mixer(pool_l, pool_w[0].astype(BF16), pool_scale[0].reshape(1, POOL_W),
                         tt=min(2048, n))

    ct0 = jnp.zeros((bsz * 2 * HEADS, STATE_ROWS, HEAD_DIM), F32)
    m0 = jnp.zeros((bsz, 2 * HEADS, HEAD_DIM), F32)
    ct1, m1 = _mlstm(qt_c, kt_c, vt_c, _gate_scan(gt_c), ct0, m0, emit_h=False)
    ht_f, ht_b = _mlstm(qt_l, kt_l, vt_l, _gate_scan(gt_l), ct1, m1, emit_h=True)

    hnorm = jnp.broadcast_to(head_norm[0].reshape(M_W, 1), (M_W, HEAD_DIM))
    return _mix_ffn(x1, ht_f, ht_b, ot_l, y_pool, mod_all, npre, npost, hnorm,
                    w_out[0].astype(BF16), v1a, v1b, v2, tm=tm)
```

```python
import functools

import jax
import jax.numpy as jnp
from jax import lax
from jax.experimental import pallas as pl
from jax.experimental.pallas import tpu as pltpu

F32 = jnp.float32
BF16 = jnp.bfloat16

EPS = 1e-6
N_MOD = 9
HEADS = 4
HEAD_DIM = 128
M_W = HEADS * HEAD_DIM
POOL_WINDOWS = (2, 4, 8, 16)
POOL_GW = 128
POOL_W = POOL_GW * len(POOL_WINDOWS)
GRID_W = 64
POOL_PAD = 8
CHUNK = 256
GATE_COLS = 4 * HEADS
FF_CHUNK = 256
STATE_ROWS = HEAD_DIM + 16
MLSTM_CHUNKS_PER_STEP = 2
V7X_VMEM_LIMIT = 56 * 1024 * 1024

_NT = (((1,), (1,)), ((), ()))
_TN = (((0,), (0,)), ((), ()))


def _sigmoid(x):
    return 1.0 / (1.0 + jnp.exp(-x))


def _log_sigmoid(x):
    return jnp.minimum(x, 0.0) - jnp.log(1.0 + jnp.exp(-jnp.abs(x)))


def _rms(x, g):
    ms = jnp.mean(x * x, axis=-1, keepdims=True)
    return x * lax.rsqrt(ms + EPS) * g


def _dot(a, b, dims=None):
    if dims is None:
        return jnp.dot(a, b, preferred_element_type=F32)
    return lax.dot_general(a, b, dims, preferred_element_type=F32)


def _const_spec(shape):
    nd = len(shape)
    return pl.BlockSpec(shape, lambda *_: (0,) * nd, pipeline_mode=pl.Buffered(1))


def _params(sem):
    return pltpu.CompilerParams(dimension_semantics=sem, vmem_limit_bytes=V7X_VMEM_LIMIT)


def _mod_kernel(s_ref, w_ref, b_ref, o_ref):
    s = s_ref[...]
    act = (s * _sigmoid(s)).astype(BF16)
    o_ref[...] = _dot(act, w_ref[...].astype(BF16)) + b_ref[...]


def _modulation(s_rows, w_mod, b_mod):
    rows, d = s_rows.shape
    n = w_mod.shape[1]
    tn = 1024
    return pl.pallas_call(
        _mod_kernel,
        grid=(n // tn,),
        in_specs=[pl.BlockSpec((rows, d), lambda j: (0, 0)),
                  pl.BlockSpec((d, tn), lambda j: (0, j)),
                  pl.BlockSpec((1, tn), lambda j: (0, j))],
        out_specs=pl.BlockSpec((rows, tn), lambda j: (0, j)),
        out_shape=jax.ShapeDtypeStruct((rows, n), F32),
        compiler_params=_params(("arbitrary",)),
        name="mod",
    )(s_rows, w_mod, b_mod.reshape(1, n))


def _swiglu_into(acc_ref, y_ref, w1a_ref, w1b_ref, w2_ref):
    y = y_ref[...]
    for j in range(w2_ref.shape[0] // FF_CHUNK):
        cols = slice(j * FF_CHUNK, (j + 1) * FF_CHUNK)
        a = _dot(y, w1a_ref[:, cols])
        b = _dot(y, w1b_ref[:, cols])
        h = ((a * _sigmoid(a)) * b).astype(BF16)
        part = _dot(h, w2_ref[cols, :])
        if j == 0:
            acc_ref[...] = part
        else:
            acc_ref[...] += part


def _ffn_in_kernel(with_rest, x_ref, mod_ref, npre_ref, npost_ref, w1a_ref, w1b_ref, w2_ref,
                   win_ref, wint_ref, gbias_ref, *rest):
    if with_rest:
        x1_ref, qk_ref, gt_ref, vt_ref, pool_ref, ot_ref, y_scr, acc_scr = rest
    else:
        qk_ref, gt_ref, vt_ref, y_scr, acc_scr = rest
    tm = x_ref.shape[0]
    x = x_ref[...]
    y = _rms(x, npre_ref[0:1, :]) * (1.0 + mod_ref[0, 1:2, :]) + mod_ref[0, 0:1, :]
    y_scr[...] = y.astype(BF16)
    _swiglu_into(acc_scr, y_scr, w1a_ref, w1b_ref, w2_ref)
    x1 = x + (0.5 * mod_ref[0, 2:3, :]) * _rms(acc_scr[...], npost_ref[0:1, :])
    y2 = _rms(x1, npre_ref[1:2, :]) * (1.0 + mod_ref[0, 4:5, :]) + mod_ref[0, 3:4, :]
    y_scr[...] = y2.astype(BF16)
    y2b = y_scr[...]
    qk_ref[...] = _dot(y2b, win_ref[:, 0:M_W])
    feat = _dot(wint_ref[...], y2b, _NT)
    g0, g1 = M_W, M_W + GATE_COLS
    gt_ref[...] = feat[g0:g1, :] + jnp.concatenate([gbias_ref[...]] * (tm // 128), axis=1)
    for c in range(vt_ref.shape[0]):
        vt_ref[c] = feat[0:g0, c * CHUNK:(c + 1) * CHUNK].astype(BF16)
    if with_rest:
        x1_ref[...] = x1
        pool_ref[...] = _dot(y2b, win_ref[:, M_W:M_W + POOL_W])
        for c in range(ot_ref.shape[0]):
            ot_ref[c] = feat[g1:g1 + M_W, c * CHUNK:(c + 1) * CHUNK]


def _ffn_in(x, mod_all, mod_row0, npre, npost, w1a, w1b, w2, win, wint, gbias, *, with_rest, tm):
    bsz, n, d = x.shape
    grid = (bsz, n // tm)
    tok = lambda w: pl.BlockSpec((None, tm, w), lambda b, i: (b, i, 0))
    feat = pl.BlockSpec((None, tm // CHUNK, M_W, CHUNK), lambda b, i: (b, i, 0, 0))
    gates = pl.BlockSpec((None, GATE_COLS, tm), lambda b, i: (b, 0, i))
    if mod_row0 is None:
        mod_spec = pl.BlockSpec((1, N_MOD, d), lambda b, i: (b, 0, 0))
    else:
        mod_spec = pl.BlockSpec((1, N_MOD, d), lambda b, i: (mod_row0, 0, 0))
    in_specs = [tok(d), mod_spec, _const_spec(npre.shape), _const_spec(npost.shape),
                _const_spec(w1a.shape), _const_spec(w1b.shape), _const_spec(w2.shape),
                _const_spec(win.shape), _const_spec(wint.shape), _const_spec(gbias.shape)]
    sds = lambda w, dt: jax.ShapeDtypeStruct((bsz, n, w), dt)
    fsds = lambda dt: jax.ShapeDtypeStruct((bsz, n // CHUNK, M_W, CHUNK), dt)
    out_specs = [tok(M_W), gates, feat]
    out_shape = [sds(M_W, F32), jax.ShapeDtypeStruct((bsz, GATE_COLS, n), F32), fsds(BF16)]
    if with_rest:
        out_specs = [tok(d)] + out_specs + [tok(POOL_W), feat]
        out_shape = [sds(d, F32)] + out_shape + [sds(POOL_W, F32), fsds(F32)]
    return pl.pallas_call(
        functools.partial(_ffn_in_kernel, with_rest),
        grid=grid,
        in_specs=in_specs,
        out_specs=out_specs,
        out_shape=out_shape,
        scratch_shapes=[pltpu.VMEM((tm, d), BF16), pltpu.VMEM((tm, d), F32)],
        compiler_params=_params(("parallel", "arbitrary")),
        name="ffn_in" if with_rest else "ffn_in_ctx",
    )(x, mod_all, npre, npost, w1a, w1b, w2, win, wint, gbias)


def _qk_kernel(nt, x_ref, prev_ref, next_ref, cw_ref, cb_ref, wqt_ref, wkt_ref, qt_ref, kt_ref):
    j = pl.program_id(1)
    x = x_ref[...]
    tq = x.shape[0]
    prev_row = prev_ref[7:8, :] * jnp.where(j > 0, 1.0, 0.0)
    next_row = next_ref[0:1, :] * jnp.where(j < nt - 1, 1.0, 0.0)
    row = lax.broadcasted_iota(jnp.int32, x.shape, 0)
    up = jnp.where(row == 0, prev_row, pltpu.roll(x, 1, 0))
    dn = jnp.where(row == tq - 1, next_row, pltpu.roll(x, tq - 1, 0))
    conv = cb_ref[...] + up * cw_ref[0:1, :] + x * cw_ref[1:2, :] + dn * cw_ref[2:3, :]
    u = (conv * _sigmoid(conv)).astype(BF16)
    scale = HEAD_DIM ** -0.5
    for h in range(HEADS):
        sl = slice(h * HEAD_DIM, (h + 1) * HEAD_DIM)
        uh = u[:, sl]
        qt = _dot(wqt_ref[h], uh, _NT).astype(BF16)
        kt = (_dot(wkt_ref[h], uh, _NT) * scale).astype(BF16)
        for c in range(qt_ref.shape[0]):
            qt_ref[c, sl, :] = qt[:, c * CHUNK:(c + 1) * CHUNK]
            kt_ref[c, sl, :] = kt[:, c * CHUNK:(c + 1) * CHUNK]


def _qk_proj(qk_src, conv_w, conv_b, wqt, wkt, *, tq):
    bsz, n, w = qk_src.shape
    nt = n // tq
    blk8 = tq // 8
    last8 = n // 8 - 1
    tok = pl.BlockSpec((None, tq, w), lambda b, j: (b, j, 0))
    prev = pl.BlockSpec((None, 8, w), lambda b, j: (b, jnp.maximum(j * blk8 - 1, 0), 0))
    nxt = pl.BlockSpec((None, 8, w), lambda b, j: (b, jnp.minimum((j + 1) * blk8, last8), 0))
    feat = pl.BlockSpec((None, tq // CHUNK, w, CHUNK), lambda b, j: (b, j, 0, 0))
    out = jax.ShapeDtypeStruct((bsz, n // CHUNK, w, CHUNK), BF16)
    return pl.pallas_call(
        functools.partial(_qk_kernel, nt),
        grid=(bsz, nt),
        in_specs=[tok, prev, nxt, _const_spec(conv_w.shape), _const_spec(conv_b.shape),
                  _const_spec(wqt.shape), _const_spec(wkt.shape)],
        out_specs=[feat, feat],
        out_shape=[out, out],
        compiler_params=_params(("parallel", "arbitrary")),
        name="qk",
    )(qk_src, qk_src, qk_src, conv_w, conv_b, wqt, wkt)


def _pool_kernel(nt, grid_h, x_ref, prev_ref, next_ref, wp_ref, ps_ref, y_ref, xs_ref, *bufs):
    j = pl.program_id(1)
    tt = x_ref.shape[0]
    halo = prev_ref.shape[0]
    halo_rows = halo // GRID_W
    rows = tt // GRID_W
    d0, d1 = POOL_PAD, POOL_PAD + GRID_W
    xs_ref[0:halo, :] = prev_ref[...] * jnp.where(j > 0, 1.0, 0.0)
    xs_ref[halo:halo + tt, :] = x_ref[...]
    xs_ref[halo + tt:halo + tt + halo, :] = next_ref[...] * jnp.where(j < nt - 1, 1.0, 0.0)
    zpad = jnp.zeros((rows, POOL_PAD, POOL_GW), F32)
    for buf in bufs:
        buf[:, 0:d0, :] = zpad
        buf[:, d1:d1 + POOL_PAD, :] = zpad
    c = lax.broadcasted_iota(jnp.int32, (1, GRID_W, POOL_GW), 1)
    r = j * rows + lax.broadcasted_iota(jnp.int32, (rows, 1, POOL_GW), 0)

    def shifted_sum(src, dst, k):
        dst[:, d0:d1, :] = src[:, d0:d1, :] + src[:, d0 + k:d1 + k, :]

    for g, win in enumerate(POOL_WINDOWS):
        lo, hi = win // 2, win - 1 - win // 2
        sl = slice(g * POOL_GW, (g + 1) * POOL_GW)
        first = (halo_rows - lo) * GRID_W
        vs = xs_ref[first:first + (rows + win - 1) * GRID_W, sl]
        span = 1
        while span < win:
            vs = vs[:-span * GRID_W] + vs[span * GRID_W:]
            span *= 2
        a, b, cc, dd = bufs
        a[:, d0:d1, :] = vs.reshape(rows, GRID_W, POOL_GW)
        trail, lead, spare_t, spare_l, k = a, a, b, dd, 1
        while k < lo:
            shifted_sum(trail, spare_t, -k)
            shifted_sum(lead, spare_l, k)
            trail, spare_t = spare_t, (cc if spare_t is b else b)
            lead, spare_l = spare_l, (a if spare_l is dd else dd)
            k *= 2
        tot = trail[:, d0 - 1:d1 - 1, :] + lead[:, d0:d1, :]
        cnt_c = jnp.minimum(c + hi, GRID_W - 1) - jnp.maximum(c - lo, 0) + 1
        cnt_r = jnp.minimum(r + hi, grid_h - 1) - jnp.maximum(r - lo, 0) + 1
        pg = tot * (1.0 / cnt_c.astype(F32)) * (1.0 / cnt_r.astype(F32))
        dlt = (pg.reshape(tt, POOL_GW) - x_ref[:, sl]).astype(BF16)
        y_ref[:, sl] = (_dot(dlt, wp_ref[g]) * ps_ref[:, sl]).astype(BF16)


def _pool_mixer(u, wp, ps, *, tt):
    bsz, n, w = u.shape
    nt = n // tt
    halo = 8 * GRID_W
    hb = tt // halo
    last = n // halo - 1
    tok = pl.BlockSpec((None, tt, w), lambda b, j: (b, j, 0))
    prev = pl.BlockSpec((None, halo, w), lambda b, j: (b, jnp.maximum(j * hb - 1, 0), 0))
    nxt = pl.BlockSpec((None, halo, w), lambda b, j: (b, jnp.minimum((j + 1) * hb, last), 0))
    return pl.pallas_call(
        functools.partial(_pool_kernel, nt, n // GRID_W),
        grid=(bsz, nt),
        in_specs=[tok, prev, nxt, _const_spec(wp.shape), _const_spec(ps.shape)],
        out_specs=tok,
        out_shape=jax.ShapeDtypeStruct((bsz, n, w), BF16),
        scratch_shapes=[pltpu.VMEM((tt + 2 * halo, w), F32)]
        + [pltpu.VMEM((tt // GRID_W, GRID_W + 2 * POOL_PAD, POOL_GW), F32)] * 4,
        compiler_params=_params(("parallel", "arbitrary")),
        name="pool",
    )(u, u, u, wp, ps)


def _split3(x):
    hi = x.astype(BF16).astype(F32)
    r = x - hi
    mid = r.astype(BF16).astype(F32)
    lo = (r - mid).astype(BF16).astype(F32)
    return hi, mid, lo


def _gate_scan_kernel(chunk, gt_ref, o_ref):
    n_combo = 2 * HEADS
    g = gt_ref[...]
    n = g.shape[1]
    pos = jnp.bitwise_and(lax.broadcasted_iota(jnp.int32, (n_combo, n), 1), chunk - 1)
    is_fwd_row = lax.broadcasted_iota(jnp.int32, (n_combo, n), 0) < HEADS

    def scan(x, op, ident):
        sft = 1
        while sft < chunk:
            from_l = jnp.where(pos >= sft, pltpu.roll(x, sft, 1), ident)
            from_r = jnp.where(pos < chunk - sft, pltpu.roll(x, n - sft, 1), ident)
            x = op(x, jnp.where(is_fwd_row, from_l, from_r))
            sft *= 2
        return x

    brow = scan(_log_sigmoid(g[n_combo:]), jnp.add, 0.0)
    rowb = g[:n_combo] - brow
    o_ref[0:n_combo, :] = rowb
    o_ref[n_combo:2 * n_combo, :] = brow
    o_ref[2 * n_combo:, :] = scan(rowb, jnp.maximum, -jnp.inf)


def _gate_scan(gt):
    bsz, rows, n = gt.shape
    return pl.pallas_call(
        functools.partial(_gate_scan_kernel, CHUNK),
        grid=(bsz,),
        in_specs=[pl.BlockSpec((None, rows, n), lambda b: (b, 0, 0))],
        out_specs=pl.BlockSpec((None, 3 * rows // 2, n), lambda b: (b, 0, 0)),
        out_shape=jax.ShapeDtypeStruct((bsz, 3 * rows // 2, n), F32),
        compiler_params=_params(("parallel",)),
        name="gate_scan",
    )(gt)


def _mlstm_kernel(emit_h, emit_state, *refs):
    fwd, bwd = refs[0:4], refs[4:8]
    ct0_ref, m0_ref = refs[8:10]
    outs = list(refs[10:])
    hf_ref, hb_ref = (outs.pop(0), outs.pop(0)) if emit_h else (None, None)
    cto_ref, mo_ref = (outs.pop(0), outs.pop(0)) if emit_state else (None, None)
    ct_scr, m_scr = outs
    i = pl.program_id(0)
    n_combo = 2 * HEADS
    bsz, cps, _, L = fwd[0].shape

    @pl.when(i == 0)
    def _():
        ct_scr[...] = ct0_ref[...]
        m_scr[...] = m0_ref[...]

    is_fwd_row = lax.broadcasted_iota(jnp.int32, (n_combo, L), 0) < HEADS
    is_fwd_col = lax.broadcasted_iota(jnp.int32, (n_combo, 1), 0) < HEADS
    krow = lax.broadcasted_iota(jnp.int32, (8, L), 0)
    tpos = lax.broadcasted_iota(jnp.int32, (L, L), 0)
    spos = lax.broadcasted_iota(jnp.int32, (L, L), 1)
    masks = (spos <= tpos, spos >= tpos)
    ones_rows = jnp.ones((STATE_ROWS - HEAD_DIM, L), BF16)

    def position_rows(b, slots):
        lanes = [slice(u * L, (u + 1) * L) for u in slots]
        pick = lambda r: jnp.where(is_fwd_row, fwd[3][b, r * n_combo:(r + 1) * n_combo, lanes[0]],
                                   bwd[3][b, r * n_combo:(r + 1) * n_combo, lanes[1]])
        last = lambda a: jnp.where(is_fwd_col, a[:, L - 1:L], a[:, 0:1])
        rowb, brow, cmax = pick(0), pick(1), pick(2)
        total, rowb_max = last(brow), last(cmax)
        m_in = m_scr[b][:, 0:1]
        mm = jnp.maximum(cmax, m_in)
        m_loc = total + rowb_max
        m_new = jnp.maximum(total + m_in, m_loc)
        m_scr[b] = jnp.broadcast_to(m_new, (n_combo, HEAD_DIM))
        return dict(w_inter=jnp.exp(m_in - mm), floor=jnp.exp(-(brow + mm)),
                    w_state=jnp.exp(rowb - rowb_max), a_old=jnp.exp(total + m_in - m_new),
                    a_new=jnp.exp(m_loc - m_new), rb3=_split3(rowb), nm3=_split3(-mm))

    def front(b, d, h, u, pr):
        c = d * HEADS + h
        sl = slice(h * HEAD_DIM, (h + 1) * HEAD_DIM)
        row = lambda a: a[c:c + 1, :]
        src = fwd if d == 0 else bwd
        qt, kt, vt = src[0][b, u, sl, :], src[1][b, u, sl, :], src[2][b, u, sl, :]
        vext = jnp.concatenate([vt, ones_rows], axis=0)
        ct = ct_scr[b * n_combo + c]
        out = dict(b=b, d=d, u=u, sl=sl, vext=vext, w_inter=row(pr["w_inter"]),
                   floor=row(pr["floor"]))
        if emit_h:
            nm3, rb3 = pr["nm3"], pr["rb3"]
            lhs = jnp.where(krow < 3, 1.0, jnp.where(krow == 3, row(nm3[0]), jnp.where(
                krow == 4, row(nm3[1]), jnp.where(krow == 5, row(nm3[2]), 0.0))))
            rhs = jnp.where(krow == 0, row(rb3[0]), jnp.where(krow == 1, row(rb3[1]), jnp.where(
                krow == 2, row(rb3[2]), jnp.where(krow < 6, 1.0, 0.0))))
            out["expo"] = _dot(lhs.astype(BF16), rhs.astype(BF16), _TN)
            out["s0"] = _dot(qt, kt, _TN)
            out["state_t"] = _dot(ct.astype(BF16), qt)
        ktw = (kt.astype(F32) * row(pr["w_state"])).astype(BF16)
        ckv = _dot(vext, ktw, _NT)
        ct_scr[b * n_combo + c] = pr["a_old"][c:c + 1, :] * ct + pr["a_new"][c:c + 1, :] * ckv
        return out

    def back(f):
        dmat = jnp.where(masks[f["d"]], jnp.exp(f["expo"]), 0.0)
        s = (f["s0"] * dmat).astype(BF16)
        res = f["w_inter"] * f["state_t"] + _dot(f["vext"], s, _NT)
        den = res[HEAD_DIM:HEAD_DIM + 1, :]
        h_ref = hf_ref if f["d"] == 0 else hb_ref
        h_ref[f["b"], f["u"], f["sl"], :] = res[:HEAD_DIM] / jnp.maximum(jnp.abs(den), f["floor"])

    pending = None
    for step in range(cps):
        slots = (step, cps - 1 - step)
        for b in range(bsz):
            pr = position_rows(b, slots)
            for d in range(2):
                for h in range(HEADS):
                    cur = front(b, d, h, slots[d], pr)
                    if emit_h and pending is not None:
                        back(pending)
                    pending = cur
    if emit_h:
        back(pending)

    if emit_state:
        @pl.when(i == pl.num_programs(0) - 1)
        def _():
            cto_ref[...] = ct_scr[...]
            mo_ref[...] = m_scr[...]


def _mlstm(qt, kt, vt, sc, ct0, m0, *, emit_h):
    bsz, nc, w, _ = qt.shape
    cps = MLSTM_CHUNKS_PER_STEP if nc % MLSTM_CHUNKS_PER_STEP == 0 else 1
    steps = nc // cps
    fpos = lambda i: i
    bpos = lambda i: steps - 1 - i

    def feat(pos_fn):
        return pl.BlockSpec((bsz, cps, w, CHUNK), lambda i: (0, pos_fn(i), 0, 0))

    def specs(pos_fn):
        scs = pl.BlockSpec((bsz, sc.shape[1], cps * CHUNK), lambda i: (0, 0, pos_fn(i)))
        return [feat(pos_fn)] * 3 + [scs]

    ct_spec = pl.BlockSpec(ct0.shape, lambda i: (0, 0, 0))
    m_spec = pl.BlockSpec(m0.shape, lambda i: (0, 0, 0))
    if emit_h:
        hsd = jax.ShapeDtypeStruct((bsz, nc, w, CHUNK), F32)
        out_specs = [feat(fpos), feat(bpos)]
        out_shape = [hsd, hsd]
    else:
        out_specs = [ct_spec, m_spec]
        out_shape = [jax.ShapeDtypeStruct(ct0.shape, F32), jax.ShapeDtypeStruct(m0.shape, F32)]
    return pl.pallas_call(
        functools.partial(_mlstm_kernel, emit_h, not emit_h),
        grid=(steps,),
        in_specs=specs(fpos) + specs(bpos) + [ct_spec, m_spec],
        out_specs=out_specs,
        out_shape=out_shape,
        scratch_shapes=[pltpu.VMEM(ct0.shape, F32), pltpu.VMEM(m0.shape, F32)],
        compiler_params=_params(("arbitrary",)),
        name="mlstm" if emit_h else "mlstm_ctx",
    )(qt, kt, vt, sc, qt, kt, vt, sc, ct0, m0)


def _mix_ffn_kernel(x1_ref, hf_ref, hb_ref, ot_ref, yp_ref, mod_ref, npre_ref, npost_ref, hn_ref,
                    wout_ref, w1a_ref, w1b_ref, w2_ref, out_ref, mixt_scr, y_scr, acc_scr):
    hnorm = jnp.concatenate([hn_ref[...]] * (CHUNK // HEAD_DIM), axis=1)
    for c in range(hf_ref.shape[0]):
        for h in range(HEADS):
            sl = slice(h * HEAD_DIM, (h + 1) * HEAD_DIM)
            hh = hf_ref[c, sl, :] + hb_ref[c, sl, :]
            inv = lax.rsqrt(jnp.mean(hh * hh, axis=0, keepdims=True) + EPS)
            gated = hh * inv * hnorm[sl, :] * _sigmoid(ot_ref[c, sl, :])
            mixt_scr[sl, c * CHUNK:(c + 1) * CHUNK] = gated.astype(BF16)
    z = _dot(mixt_scr[...], wout_ref[0:M_W, :], _TN) + _dot(yp_ref[...], wout_ref[M_W:, :])
    x2 = x1_ref[...] + (1.0 * mod_ref[0, 5:6, :]) * _rms(z, npost_ref[1:2, :])
    y = _rms(x2, npre_ref[2:3, :]) * (1.0 + mod_ref[0, 7:8, :]) + mod_ref[0, 6:7, :]
    y_scr[...] = y.astype(BF16)
    _swiglu_into(acc_scr, y_scr, w1a_ref, w1b_ref, w2_ref)
    out_ref[...] = x2 + (0.5 * mod_ref[0, 8:9, :]) * _rms(acc_scr[...], npost_ref[2:3, :])


def _mix_ffn(x1, hf, hb, ot, yp, mod_all, npre, npost, hnorm, wout, w1a, w1b, w2, *, tm):
    bsz, n, d = x1.shape
    tok = lambda w: pl.BlockSpec((None, tm, w), lambda b, i: (b, i, 0))
    feat = pl.BlockSpec((None, tm // CHUNK, M_W, CHUNK), lambda b, i: (b, i, 0, 0))
    in_specs = [tok(d), feat, feat, feat, tok(POOL_W),
                pl.BlockSpec((1, N_MOD, d), lambda b, i: (b, 0, 0)),
                _const_spec(npre.shape), _const_spec(npost.shape), _const_spec(hnorm.shape),
                _const_spec(wout.shape), _const_spec(w1a.shape), _const_spec(w1b.shape),
                _const_spec(w2.shape)]
    return pl.pallas_call(
        _mix_ffn_kernel,
        grid=(bsz, n // tm),
        in_specs=in_specs,
        out_specs=tok(d),
        out_shape=jax.ShapeDtypeStruct((bsz, n, d), F32),
        scratch_shapes=[pltpu.VMEM((M_W, tm), BF16), pltpu.VMEM((tm, d), BF16),
                        pltpu.VMEM((tm, d), F32)],
        compiler_params=_params(("parallel", "arbitrary")),
        name="mix_ffn",
    )(x1, hf, hb, ot, yp, mod_all, npre, npost, hnorm, wout, w1a, w1b, w2)


def _ffn_weights(w_in, w_out):
    ff = w_in.shape[1] // 2
    return w_in[:, :ff].astype(BF16), w_in[:, ff:].astype(BF16), w_out.astype(BF16)


def kernel(x, c, ctx, c_ctx, w_mod, b_mod, norm_pre, norm_post, ffn_w_in, ffn_w_out, w_in, w_out,
           conv_w, conv_b, w_q, w_k, i_bias, f_bias, head_norm, pool_w, pool_scale):
    assert w_mod.shape[0] == 1, "single-layer stack"
    bsz, n, d = x.shape
    n_ctx = ctx.shape[1]
    m_cols = 3 * M_W + GATE_COLS

    rows = 8
    s_rows = jnp.zeros((rows, d), F32).at[:bsz].set(c).at[bsz].set(c_ctx)
    mod_all = _modulation(s_rows, w_mod[0], b_mod[0]).reshape(rows, N_MOD, d)

    w1a, w1b, w2 = _ffn_weights(ffn_w_in[0, 0], ffn_w_out[0, 0])
    v1a, v1b, v2 = _ffn_weights(ffn_w_in[0, 1], ffn_w_out[0, 1])
    wi = w_in[0]
    win_lat = jnp.concatenate([wi[:, :M_W], wi[:, m_cols:]], axis=1).astype(BF16)
    win_ctx = win_lat[:, :M_W]
    wint_lat = jnp.concatenate([wi[:, M_W:2 * M_W], wi[:, 3 * M_W:m_cols], wi[:, 2 * M_W:3 * M_W]],
                               axis=1).T.astype(BF16)
    wint_ctx = wint_lat[:M_W + GATE_COLS]
    gbias = jnp.broadcast_to(
        jnp.concatenate([i_bias[0].reshape(-1), f_bias[0].reshape(-1)]).reshape(GATE_COLS, 1),
        (GATE_COLS, 128))
    npre, npost = norm_pre[0], norm_post[0]

    tm = min(512, n)
    x1, qk_l, gt_l, vt_l, pool_l, ot_l = _ffn_in(x, mod_all, None, npre, npost, w1a, w1b, w2,
                                                 win_lat, wint_lat, gbias, with_rest=True, tm=tm)
    qk_c, gt_c, vt_c = _ffn_in(ctx, mod_all, bsz, npre, npost, w1a, w1b, w2, win_ctx, wint_ctx,
                               gbias, with_rest=False, tm=min(512, n_ctx))

    cb = conv_b[0].reshape(1, M_W)
    wqt = jnp.swapaxes(w_q[0], 1, 2).astype(BF16)
    wkt = jnp.swapaxes(w_k[0], 1, 2).astype(BF16)
    qt_l, kt_l = _qk_proj(qk_l, conv_w[0], cb, wqt, wkt, tq=min(1024, n))
    qt_c, kt_c = _qk_proj(qk_c, conv_w[0], cb, wqt, wkt, tq=min(1024, n_ctx))

    y_pool = _pool_mixer(pool_l, pool_w[0].astype(BF16), pool_scale[0].reshape(1, POOL_W),
                         tt=min(4096, n))

    ct0 = jnp.zeros((bsz * 2 * HEADS, STATE_ROWS, HEAD_DIM), F32)
    m0 = jnp.zeros((bsz, 2 * HEADS, HEAD_DIM), F32)
    ct1, m1 = _mlstm(qt_c, kt_c, vt_c, _gate_scan(gt_c), ct0, m0, emit_h=False)
    ht_f, ht_b = _mlstm(qt_l, kt_l, vt_l, _gate_scan(gt_l), ct1, m1, emit_h=True)

    hnorm = jnp.broadcast_to(head_norm[0].reshape(M_W, 1), (M_W, HEAD_DIM))
    return _mix_ffn(x1, ht_f, ht_b, ot_l, y_pool, mod_all, npre, npost, hnorm,
                    w_out[0].astype(BF16), v1a, v1b, v2, tm=tm)
```

```python
import functools

import jax
import jax.numpy as jnp
from jax import lax
from jax.experimental import pallas as pl
from jax.experimental.pallas import tpu as pltpu

F32 = jnp.float32
BF16 = jnp.bfloat16

EPS = 1e-6
N_MOD = 9
HEADS = 4
HEAD_DIM = 128
M_W = HEADS * HEAD_DIM
POOL_WINDOWS = (2, 4, 8, 16)
POOL_GW = 128
POOL_W = POOL_GW * len(POOL_WINDOWS)
GRID_W = 64
POOL_PAD = 8
CHUNK = 256
GATE_COLS = 4 * HEADS
FF_CHUNK = 256
STATE_ROWS = HEAD_DIM + 16
MLSTM_CHUNKS_PER_STEP = 4
V7X_VMEM_LIMIT = 56 * 1024 * 1024

_NT = (((1,), (1,)), ((), ()))
_TN = (((0,), (0,)), ((), ()))


def _sigmoid(x):
    return 1.0 / (1.0 + jnp.exp(-x))


def _log_sigmoid(x):
    return jnp.minimum(x, 0.0) - jnp.log(1.0 + jnp.exp(-jnp.abs(x)))


def _unit_rms(x):
    return x * lax.rsqrt(jnp.mean(x * x, axis=-1, keepdims=True) + EPS)


def _ada_in(x, gain, scale, shift):
    return _unit_rms(x) * (gain * (1.0 + scale)) + shift


def _ada_out(x, z, gain, gate):
    return x + _unit_rms(z) * (gain * gate)


def _dot(a, b, dims=None):
    if dims is None:
        return jnp.dot(a, b, preferred_element_type=F32)
    return lax.dot_general(a, b, dims, preferred_element_type=F32)


def _const_spec(shape):
    nd = len(shape)
    return pl.BlockSpec(shape, lambda *_: (0,) * nd, pipeline_mode=pl.Buffered(1))


def _params(sem):
    return pltpu.CompilerParams(dimension_semantics=sem, vmem_limit_bytes=V7X_VMEM_LIMIT)


def _mod_kernel(s_ref, w_ref, b_ref, o_ref):
    s = s_ref[...]
    act = (s * _sigmoid(s)).astype(BF16)
    o_ref[...] = _dot(act, w_ref[...].astype(BF16)) + b_ref[...]


def _modulation(s_rows, w_mod, b_mod):
    rows, d = s_rows.shape
    n = w_mod.shape[1]
    tn = 1024
    return pl.pallas_call(
        _mod_kernel,
        grid=(n // tn,),
        in_specs=[pl.BlockSpec((rows, d), lambda j: (0, 0)),
                  pl.BlockSpec((d, tn), lambda j: (0, j)),
                  pl.BlockSpec((1, tn), lambda j: (0, j))],
        out_specs=pl.BlockSpec((rows, tn), lambda j: (0, j)),
        out_shape=jax.ShapeDtypeStruct((rows, n), F32),
        compiler_params=_params(("arbitrary",)),
        name="mod",
    )(s_rows, w_mod, b_mod.reshape(1, n))


def _swiglu_into(acc_ref, y_ref, w1a_ref, w1b_ref, w2_ref):
    y = y_ref[...]
    for j in range(w2_ref.shape[0] // FF_CHUNK):
        cols = slice(j * FF_CHUNK, (j + 1) * FF_CHUNK)
        a = _dot(y, w1a_ref[:, cols])
        b = _dot(y, w1b_ref[:, cols])
        h = ((a * _sigmoid(a)) * b).astype(BF16)
        part = _dot(h, w2_ref[cols, :])
        if j == 0:
            acc_ref[...] = part
        else:
            acc_ref[...] += part


def _ffn_in_kernel(with_rest, x_ref, mod_ref, npre_ref, npost_ref, w1a_ref, w1b_ref, w2_ref,
                   win_ref, wint_ref, gbias_ref, *rest):
    if with_rest:
        x1_ref, qk_ref, gt_ref, vt_ref, pool_ref, ot_ref, y_scr, acc_scr = rest
    else:
        qk_ref, gt_ref, vt_ref, y_scr, acc_scr = rest
    tm = x_ref.shape[0]
    x = x_ref[...]
    y = _ada_in(x, npre_ref[0:1, :], mod_ref[0, 1:2, :], mod_ref[0, 0:1, :])
    y_scr[...] = y.astype(BF16)
    _swiglu_into(acc_scr, y_scr, w1a_ref, w1b_ref, w2_ref)
    x1 = _ada_out(x, acc_scr[...], npost_ref[0:1, :], 0.5 * mod_ref[0, 2:3, :])
    y2 = _ada_in(x1, npre_ref[1:2, :], mod_ref[0, 4:5, :], mod_ref[0, 3:4, :])
    y_scr[...] = y2.astype(BF16)
    y2b = y_scr[...]
    qk_ref[...] = _dot(y2b, win_ref[:, 0:M_W])
    feat = _dot(wint_ref[...], y2b, _NT)
    g0, g1 = M_W, M_W + GATE_COLS
    gt_ref[...] = feat[g0:g1, :] + jnp.concatenate([gbias_ref[...]] * (tm // 128), axis=1)
    for c in range(vt_ref.shape[0]):
        vt_ref[c] = feat[0:g0, c * CHUNK:(c + 1) * CHUNK].astype(BF16)
    if with_rest:
        x1_ref[...] = x1
        pool_ref[...] = _dot(y2b, win_ref[:, M_W:M_W + POOL_W])
        for c in range(ot_ref.shape[0]):
            ot_ref[c] = feat[g1:g1 + M_W, c * CHUNK:(c + 1) * CHUNK]


def _ffn_in(x, mod_all, mod_row0, npre, npost, w1a, w1b, w2, win, wint, gbias, *, with_rest, tm):
    bsz, n, d = x.shape
    grid = (bsz, n // tm)
    tok = lambda w: pl.BlockSpec((None, tm, w), lambda b, i: (b, i, 0))
    feat = pl.BlockSpec((None, tm // CHUNK, M_W, CHUNK), lambda b, i: (b, i, 0, 0))
    gates = pl.BlockSpec((None, GATE_COLS, tm), lambda b, i: (b, 0, i))
    if mod_row0 is None:
        mod_spec = pl.BlockSpec((1, N_MOD, d), lambda b, i: (b, 0, 0))
    else:
        mod_spec = pl.BlockSpec((1, N_MOD, d), lambda b, i: (mod_row0, 0, 0))
    in_specs = [tok(d), mod_spec, _const_spec(npre.shape), _const_spec(npost.shape),
                _const_spec(w1a.shape), _const_spec(w1b.shape), _const_spec(w2.shape),
                _const_spec(win.shape), _const_spec(wint.shape), _const_spec(gbias.shape)]
    sds = lambda w, dt: jax.ShapeDtypeStruct((bsz, n, w), dt)
    fsds = lambda dt: jax.ShapeDtypeStruct((bsz, n // CHUNK, M_W, CHUNK), dt)
    out_specs = [tok(M_W), gates, feat]
    out_shape = [sds(M_W, F32), jax.ShapeDtypeStruct((bsz, GATE_COLS, n), F32), fsds(BF16)]
    if with_rest:
        out_specs = [tok(d)] + out_specs + [tok(POOL_W), feat]
        out_shape = [sds(d, F32)] + out_shape + [sds(POOL_W, F32), fsds(F32)]
    return pl.pallas_call(
        functools.partial(_ffn_in_kernel, with_rest),
        grid=grid,
        in_specs=in_specs,
        out_specs=out_specs,
        out_shape=out_shape,
        scratch_shapes=[pltpu.VMEM((tm, d), BF16), pltpu.VMEM((tm, d), F32)],
        compiler_params=_params(("parallel", "arbitrary")),
        name="ffn_in" if with_rest else "ffn_in_ctx",
    )(x, mod_all, npre, npost, w1a, w1b, w2, win, wint, gbias)


def _qk_kernel(nt, x_ref, prev_ref, next_ref, cw_ref, cb_ref, wqt_ref, wkt_ref, qt_ref, kt_ref):
    j = pl.program_id(1)
    x = x_ref[...]
    tq = x.shape[0]
    prev_row = prev_ref[7:8, :] * jnp.where(j > 0, 1.0, 0.0)
    next_row = next_ref[0:1, :] * jnp.where(j < nt - 1, 1.0, 0.0)
    row = lax.broadcasted_iota(jnp.int32, x.shape, 0)
    up = jnp.where(row == 0, prev_row, pltpu.roll(x, 1, 0))
    dn = jnp.where(row == tq - 1, next_row, pltpu.roll(x, tq - 1, 0))
    conv = cb_ref[...] + up * cw_ref[0:1, :] + x * cw_ref[1:2, :] + dn * cw_ref[2:3, :]
    u = (conv * _sigmoid(conv)).astype(BF16)
    scale = HEAD_DIM ** -0.5
    for h in range(HEADS):
        sl = slice(h * HEAD_DIM, (h + 1) * HEAD_DIM)
        uh = u[:, sl]
        qt = _dot(wqt_ref[h], uh, _NT).astype(BF16)
        kt = (_dot(wkt_ref[h], uh, _NT) * scale).astype(BF16)
        for c in range(qt_ref.shape[0]):
            qt_ref[c, sl, :] = qt[:, c * CHUNK:(c + 1) * CHUNK]
            kt_ref[c, sl, :] = kt[:, c * CHUNK:(c + 1) * CHUNK]


def _qk_proj(qk_src, conv_w, conv_b, wqt, wkt, *, tq):
    bsz, n, w = qk_src.shape
    nt = n // tq
    blk8 = tq // 8
    last8 = n // 8 - 1
    tok = pl.BlockSpec((None, tq, w), lambda b, j: (b, j, 0))
    prev = pl.BlockSpec((None, 8, w), lambda b, j: (b, jnp.maximum(j * blk8 - 1, 0), 0))
    nxt = pl.BlockSpec((None, 8, w), lambda b, j: (b, jnp.minimum((j + 1) * blk8, last8), 0))
    feat = pl.BlockSpec((None, tq // CHUNK, w, CHUNK), lambda b, j: (b, j, 0, 0))
    out = jax.ShapeDtypeStruct((bsz, n // CHUNK, w, CHUNK), BF16)
    return pl.pallas_call(
        functools.partial(_qk_kernel, nt),
        grid=(bsz, nt),
        in_specs=[tok, prev, nxt, _const_spec(conv_w.shape), _const_spec(conv_b.shape),
                  _const_spec(wqt.shape), _const_spec(wkt.shape)],
        out_specs=[feat, feat],
        out_shape=[out, out],
        compiler_params=_params(("parallel", "arbitrary")),
        name="qk",
    )(qk_src, qk_src, qk_src, conv_w, conv_b, wqt, wkt)


def _pool_kernel(nt, grid_h, x_ref, prev_ref, next_ref, wp_ref, ps_ref, y_ref, xs_ref, *bufs):
    j = pl.program_id(1)
    tt = x_ref.shape[0]
    halo = prev_ref.shape[0]
    halo_rows = halo // GRID_W
    rows = tt // GRID_W
    d0, d1 = POOL_PAD, POOL_PAD + GRID_W
    xs_ref[0:halo, :] = prev_ref[...] * jnp.where(j > 0, 1.0, 0.0)
    xs_ref[halo:halo + tt, :] = x_ref[...]
    xs_ref[halo + tt:halo + tt + halo, :] = next_ref[...] * jnp.where(j < nt - 1, 1.0, 0.0)
    zpad = jnp.zeros((rows, POOL_PAD, POOL_GW), F32)
    for buf in bufs:
        buf[:, 0:d0, :] = zpad
        buf[:, d1:d1 + POOL_PAD, :] = zpad
    c = lax.broadcasted_iota(jnp.int32, (1, GRID_W, POOL_GW), 1)
    r = j * rows + lax.broadcasted_iota(jnp.int32, (rows, 1, POOL_GW), 0)

    def shifted_sum(src, dst, k):
        dst[:, d0:d1, :] = src[:, d0:d1, :] + src[:, d0 + k:d1 + k, :]

    for g, win in enumerate(POOL_WINDOWS):
        lo, hi = win // 2, win - 1 - win // 2
        sl = slice(g * POOL_GW, (g + 1) * POOL_GW)
        first = (halo_rows - lo) * GRID_W
        vs = xs_ref[first:first + (rows + win - 1) * GRID_W, sl]
        span = 1
        while span < win:
            vs = vs[:-span * GRID_W] + vs[span * GRID_W:]
            span *= 2
        a, b, cc, dd = bufs
        a[:, d0:d1, :] = vs.reshape(rows, GRID_W, POOL_GW)
        trail, lead, spare_t, spare_l, k = a, a, b, dd, 1
        while k < lo:
            shifted_sum(trail, spare_t, -k)
            shifted_sum(lead, spare_l, k)
            trail, spare_t = spare_t, (cc if spare_t is b else b)
            lead, spare_l = spare_l, (a if spare_l is dd else dd)
            k *= 2
        tot = trail[:, d0 - 1:d1 - 1, :] + lead[:, d0:d1, :]
        cnt_c = jnp.minimum(c + hi, GRID_W - 1) - jnp.maximum(c - lo, 0) + 1
        cnt_r = jnp.minimum(r + hi, grid_h - 1) - jnp.maximum(r - lo, 0) + 1
        pg = tot * (1.0 / cnt_c.astype(F32)) * (1.0 / cnt_r.astype(F32))
        dlt = (pg.reshape(tt, POOL_GW) - x_ref[:, sl]).astype(BF16)
        y_ref[:, sl] = (_dot(dlt, wp_ref[g]) * ps_ref[:, sl]).astype(BF16)


def _pool_mixer(u, wp, ps, *, tt):
    bsz, n, w = u.shape
    nt = n // tt
    halo = 8 * GRID_W
    hb = tt // halo
    last = n // halo - 1
    tok = pl.BlockSpec((None, tt, w), lambda b, j: (b, j, 0))
    prev = pl.BlockSpec((None, halo, w), lambda b, j: (b, jnp.maximum(j * hb - 1, 0), 0))
    nxt = pl.BlockSpec((None, halo, w), lambda b, j: (b, jnp.minimum((j + 1) * hb, last), 0))
    return pl.pallas_call(
        functools.partial(_pool_kernel, nt, n // GRID_W),
        grid=(bsz, nt),
        in_specs=[tok, prev, nxt, _const_spec(wp.shape), _const_spec(ps.shape)],
        out_specs=tok,
        out_shape=jax.ShapeDtypeStruct((bsz, n, w), BF16),
        scratch_shapes=[pltpu.VMEM((tt + 2 * halo, w), F32)]
        + [pltpu.VMEM((tt // GRID_W, GRID_W + 2 * POOL_PAD, POOL_GW), F32)] * 4,
        compiler_params=_params(("parallel", "arbitrary")),
        name="pool",
    )(u, u, u, wp, ps)


def _split3(x):
    hi = x.astype(BF16).astype(F32)
    r = x - hi
    mid = r.astype(BF16).astype(F32)
    lo = (r - mid).astype(BF16).astype(F32)
    return hi, mid, lo


def _gate_scan_kernel(chunk, gt_ref, o_ref):
    n_combo = 2 * HEADS
    g = gt_ref[...]
    n = g.shape[1]
    pos = jnp.bitwise_and(lax.broadcasted_iota(jnp.int32, (n_combo, n), 1), chunk - 1)
    is_fwd_row = lax.broadcasted_iota(jnp.int32, (n_combo, n), 0) < HEADS

    def scan(x, op, ident):
        sft = 1
        while sft < chunk:
            from_l = jnp.where(pos >= sft, pltpu.roll(x, sft, 1), ident)
            from_r = jnp.where(pos < chunk - sft, pltpu.roll(x, n - sft, 1), ident)
            x = op(x, jnp.where(is_fwd_row, from_l, from_r))
            sft *= 2
        return x

    brow = scan(_log_sigmoid(g[n_combo:]), jnp.add, 0.0)
    rowb = g[:n_combo] - brow
    o_ref[0:n_combo, :] = rowb
    o_ref[n_combo:2 * n_combo, :] = brow
    o_ref[2 * n_combo:, :] = scan(rowb, jnp.maximum, -jnp.inf)


def _gate_scan(gt):
    bsz, rows, n = gt.shape
    return pl.pallas_call(
        functools.partial(_gate_scan_kernel, CHUNK),
        grid=(bsz,),
        in_specs=[pl.BlockSpec((None, rows, n), lambda b: (b, 0, 0))],
        out_specs=pl.BlockSpec((None, 3 * rows // 2, n), lambda b: (b, 0, 0)),
        out_shape=jax.ShapeDtypeStruct((bsz, 3 * rows // 2, n), F32),
        compiler_params=_params(("parallel",)),
        name="gate_scan",
    )(gt)


def _mlstm_kernel(emit_h, emit_state, *refs):
    fwd, bwd = refs[0:4], refs[4:8]
    ct0_ref, m0_ref = refs[8:10]
    outs = list(refs[10:])
    hf_ref, hb_ref = (outs.pop(0), outs.pop(0)) if emit_h else (None, None)
    cto_ref, mo_ref = (outs.pop(0), outs.pop(0)) if emit_state else (None, None)
    ct_scr, m_scr = outs
    i = pl.program_id(0)
    n_combo = 2 * HEADS
    bsz, cps, _, L = fwd[0].shape

    @pl.when(i == 0)
    def _():
        ct_scr[...] = ct0_ref[...]
        m_scr[...] = m0_ref[...]

    is_fwd_row = lax.broadcasted_iota(jnp.int32, (n_combo, L), 0) < HEADS
    is_fwd_col = lax.broadcasted_iota(jnp.int32, (n_combo, 1), 0) < HEADS
    krow = lax.broadcasted_iota(jnp.int32, (8, L), 0)
    tpos = lax.broadcasted_iota(jnp.int32, (L, L), 0)
    spos = lax.broadcasted_iota(jnp.int32, (L, L), 1)
    masks = (spos <= tpos, spos >= tpos)
    ones_rows = jnp.ones((STATE_ROWS - HEAD_DIM, L), BF16)

    def position_rows(b, slots):
        lanes = [slice(u * L, (u + 1) * L) for u in slots]
        pick = lambda r: jnp.where(is_fwd_row, fwd[3][b, r * n_combo:(r + 1) * n_combo, lanes[0]],
                                   bwd[3][b, r * n_combo:(r + 1) * n_combo, lanes[1]])
        last = lambda a: jnp.where(is_fwd_col, a[:, L - 1:L], a[:, 0:1])
        rowb, brow, cmax = pick(0), pick(1), pick(2)
        total, rowb_max = last(brow), last(cmax)
        m_in = m_scr[b][:, 0:1]
        mm = jnp.maximum(cmax, m_in)
        m_loc = total + rowb_max
        m_new = jnp.maximum(total + m_in, m_loc)
        m_scr[b] = jnp.broadcast_to(m_new, (n_combo, HEAD_DIM))
        return dict(w_inter=jnp.exp(m_in - mm), floor=jnp.exp(-(brow + mm)),
                    w_state=jnp.exp(rowb - rowb_max), a_old=jnp.exp(total + m_in - m_new),
                    a_new=jnp.exp(m_loc - m_new), rb3=_split3(rowb), nm3=_split3(-mm))

    def front(b, d, h, u, pr):
        c = d * HEADS + h
        sl = slice(h * HEAD_DIM, (h + 1) * HEAD_DIM)
        row = lambda a: a[c:c + 1, :]
        src = fwd if d == 0 else bwd
        qt, kt, vt = src[0][b, u, sl, :], src[1][b, u, sl, :], src[2][b, u, sl, :]
        vext = jnp.concatenate([vt, ones_rows], axis=0)
        ct = ct_scr[b * n_combo + c]
        out = dict(b=b, d=d, u=u, sl=sl, vext=vext, w_inter=row(pr["w_inter"]),
                   floor=row(pr["floor"]))
        if emit_h:
            nm3, rb3 = pr["nm3"], pr["rb3"]
            lhs = jnp.where(krow < 3, 1.0, jnp.where(krow == 3, row(nm3[0]), jnp.where(
                krow == 4, row(nm3[1]), jnp.where(krow == 5, row(nm3[2]), 0.0))))
            rhs = jnp.where(krow == 0, row(rb3[0]), jnp.where(krow == 1, row(rb3[1]), jnp.where(
                krow == 2, row(rb3[2]), jnp.where(krow < 6, 1.0, 0.0))))
            out["expo"] = _dot(lhs.astype(BF16), rhs.astype(BF16), _TN)
            out["s0"] = _dot(qt, kt, _TN)
            out["state_t"] = _dot(ct.astype(BF16), qt)
        ktw = (kt.astype(F32) * row(pr["w_state"])).astype(BF16)
        ckv = _dot(vext, ktw, _NT)
        ct_scr[b * n_combo + c] = pr["a_old"][c:c + 1, :] * ct + pr["a_new"][c:c + 1, :] * ckv
        return out

    def back(f):
        dmat = jnp.where(masks[f["d"]], jnp.exp(f["expo"]), 0.0)
        s = (f["s0"] * dmat).astype(BF16)
        res = f["w_inter"] * f["state_t"] + _dot(f["vext"], s, _NT)
        den = res[HEAD_DIM:HEAD_DIM + 1, :]
        h_ref = hf_ref if f["d"] == 0 else hb_ref
        h_ref[f["b"], f["u"], f["sl"], :] = res[:HEAD_DIM] / jnp.maximum(jnp.abs(den), f["floor"])

    pending = None
    for step in range(cps):
        slots = (step, cps - 1 - step)
        for b in range(bsz):
            pr = position_rows(b, slots)
            for d in range(2):
                for h in range(HEADS):
                    cur = front(b, d, h, slots[d], pr)
                    if emit_h and pending is not None:
                        back(pending)
                    pending = cur
    if emit_h:
        back(pending)

    if emit_state:
        @pl.when(i == pl.num_programs(0) - 1)
        def _():
            cto_ref[...] = ct_scr[...]
            mo_ref[...] = m_scr[...]


def _mlstm(qt, kt, vt, sc, ct0, m0, *, emit_h):
    bsz, nc, w, _ = qt.shape
    cps = MLSTM_CHUNKS_PER_STEP if nc % MLSTM_CHUNKS_PER_STEP == 0 else 1
    steps = nc // cps
    fpos = lambda i: i
    bpos = lambda i: steps - 1 - i

    def feat(pos_fn):
        return pl.BlockSpec((bsz, cps, w, CHUNK), lambda i: (0, pos_fn(i), 0, 0))

    def specs(pos_fn):
        scs = pl.BlockSpec((bsz, sc.shape[1], cps * CHUNK), lambda i: (0, 0, pos_fn(i)))
        return [feat(pos_fn)] * 3 + [scs]

    ct_spec = pl.BlockSpec(ct0.shape, lambda i: (0, 0, 0))
    m_spec = pl.BlockSpec(m0.shape, lambda i: (0, 0, 0))
    if emit_h:
        hsd = jax.ShapeDtypeStruct((bsz, nc, w, CHUNK), F32)
        out_specs = [feat(fpos), feat(bpos)]
        out_shape = [hsd, hsd]
    else:
        out_specs = [ct_spec, m_spec]
        out_shape = [jax.ShapeDtypeStruct(ct0.shape, F32), jax.ShapeDtypeStruct(m0.shape, F32)]
    return pl.pallas_call(
        functools.partial(_mlstm_kernel, emit_h, not emit_h),
        grid=(steps,),
        in_specs=specs(fpos) + specs(bpos) + [ct_spec, m_spec],
        out_specs=out_specs,
        out_shape=out_shape,
        scratch_shapes=[pltpu.VMEM(ct0.shape, F32), pltpu.VMEM(m0.shape, F32)],
        compiler_params=_params(("arbitrary",)),
        name="mlstm" if emit_h else "mlstm_ctx",
    )(qt, kt, vt, sc, qt, kt, vt, sc, ct0, m0)


def _mix_ffn_kernel(x1_ref, hf_ref, hb_ref, ot_ref, yp_ref, mod_ref, npre_ref, npost_ref, hn_ref,
                    wout_ref, w1a_ref, w1b_ref, w2_ref, out_ref, mixt_scr, y_scr, acc_scr):
    hnorm = jnp.concatenate([hn_ref[...]] * (CHUNK // HEAD_DIM), axis=1)
    for c in range(hf_ref.shape[0]):
        for h in range(HEADS):
            sl = slice(h * HEAD_DIM, (h + 1) * HEAD_DIM)
            hh = hf_ref[c, sl, :] + hb_ref[c, sl, :]
            inv = lax.rsqrt(jnp.mean(hh * hh, axis=0, keepdims=True) + EPS)
            gated = hh * inv * hnorm[sl, :] * _sigmoid(ot_ref[c, sl, :])
            mixt_scr[sl, c * CHUNK:(c + 1) * CHUNK] = gated.astype(BF16)
    z = _dot(mixt_scr[...], wout_ref[0:M_W, :], _TN) + _dot(yp_ref[...], wout_ref[M_W:, :])
    x2 = _ada_out(x1_ref[...], z, npost_ref[1:2, :], 1.0 * mod_ref[0, 5:6, :])
    y = _ada_in(x2, npre_ref[2:3, :], mod_ref[0, 7:8, :], mod_ref[0, 6:7, :])
    y_scr[...] = y.astype(BF16)
    _swiglu_into(acc_scr, y_scr, w1a_ref, w1b_ref, w2_ref)
    out_ref[...] = _ada_out(x2, acc_scr[...], npost_ref[2:3, :], 0.5 * mod_ref[0, 8:9, :])


def _mix_ffn(x1, hf, hb, ot, yp, mod_all, npre, npost, hnorm, wout, w1a, w1b, w2, *, tm):
    bsz, n, d = x1.shape
    tok = lambda w: pl.BlockSpec((None, tm, w), lambda b, i: (b, i, 0))
    feat = pl.BlockSpec((None, tm // CHUNK, M_W, CHUNK), lambda b, i: (b, i, 0, 0))
    in_specs = [tok(d), feat, feat, feat, tok(POOL_W),
                pl.BlockSpec((1, N_MOD, d), lambda b, i: (b, 0, 0)),
                _const_spec(npre.shape), _const_spec(npost.shape), _const_spec(hnorm.shape),
                _const_spec(wout.shape), _const_spec(w1a.shape), _const_spec(w1b.shape),
                _const_spec(w2.shape)]
    return pl.pallas_call(
        _mix_ffn_kernel,
        grid=(bsz, n // tm),
        in_specs=in_specs,
        out_specs=tok(d),
        out_shape=jax.ShapeDtypeStruct((bsz, n, d), F32),
        scratch_shapes=[pltpu.VMEM((M_W, tm), BF16), pltpu.VMEM((tm, d), BF16),
                        pltpu.VMEM((tm, d), F32)],
        compiler_params=_params(("parallel", "arbitrary")),
        name="mix_ffn",
    )(x1, hf, hb, ot, yp, mod_all, npre, npost, hnorm, wout, w1a, w1b, w2)


def _ffn_weights(w_in, w_out):
    ff = w_in.shape[1] // 2
    return w_in[:, :ff].astype(BF16), w_in[:, ff:].astype(BF16), w_out.astype(BF16)


def kernel(x, c, ctx, c_ctx, w_mod, b_mod, norm_pre, norm_post, ffn_w_in, ffn_w_out, w_in, w_out,
           conv_w, conv_b, w_q, w_k, i_bias, f_bias, head_norm, pool_w, pool_scale):
    assert w_mod.shape[0] == 1, "single-layer stack"
    bsz, n, d = x.shape
    n_ctx = ctx.shape[1]
    m_cols = 3 * M_W + GATE_COLS

    rows = 8
    s_rows = jnp.zeros((rows, d), F32).at[:bsz].set(c).at[bsz].set(c_ctx)
    mod_all = _modulation(s_rows, w_mod[0], b_mod[0]).reshape(rows, N_MOD, d)

    w1a, w1b, w2 = _ffn_weights(ffn_w_in[0, 0], ffn_w_out[0, 0])
    v1a, v1b, v2 = _ffn_weights(ffn_w_in[0, 1], ffn_w_out[0, 1])
    wi = w_in[0]
    win_lat = jnp.concatenate([wi[:, :M_W], wi[:, m_cols:]], axis=1).astype(BF16)
    win_ctx = win_lat[:, :M_W]
    wint_lat = jnp.concatenate([wi[:, M_W:2 * M_W], wi[:, 3 * M_W:m_cols], wi[:, 2 * M_W:3 * M_W]],
                               axis=1).T.astype(BF16)
    wint_ctx = wint_lat[:M_W + GATE_COLS]
    gbias = jnp.broadcast_to(
        jnp.concatenate([i_bias[0].reshape(-1), f_bias[0].reshape(-1)]).reshape(GATE_COLS, 1),
        (GATE_COLS, 128))
    npre, npost = norm_pre[0], norm_post[0]

    tm = min(512, n)
    x1, qk_l, gt_l, vt_l, pool_l, ot_l = _ffn_in(x, mod_all, None, npre, npost, w1a, w1b, w2,
                                                 win_lat, wint_lat, gbias, with_rest=True, tm=tm)
    qk_c, gt_c, vt_c = _ffn_in(ctx, mod_all, bsz, npre, npost, w1a, w1b, w2, win_ctx, wint_ctx,
                               gbias, with_rest=False, tm=min(512, n_ctx))

    cb = conv_b[0].reshape(1, M_W)
    wqt = jnp.swapaxes(w_q[0], 1, 2).astype(BF16)
    wkt = jnp.swapaxes(w_k[0], 1, 2).astype(BF16)
    qt_l, kt_l = _qk_proj(qk_l, conv_w[0], cb, wqt, wkt, tq=min(1024, n))
    qt_c, kt_c = _qk_proj(qk_c, conv_w[0], cb, wqt, wkt, tq=min(1024, n_ctx))

    y_pool = _pool_mixer(pool_l, pool_w[0].astype(BF16), pool_scale[0].reshape(1, POOL_W),
                         tt=min(4096, n))

    ct0 = jnp.zeros((bsz * 2 * HEADS, STATE_ROWS, HEAD_DIM), F32)
    m0 = jnp.zeros((bsz, 2 * HEADS, HEAD_DIM), F32)
    ct1, m1 = _mlstm(qt_c, kt_c, vt_c, _gate_scan(gt_c), ct0, m0, emit_h=False)
    ht_f, ht_b = _mlstm(qt_l, kt_l, vt_l, _gate_scan(gt_l), ct1, m1, emit_h=True)

    hnorm = jnp.broadcast_to(head_norm[0].reshape(M_W, 1), (M_W, HEAD_DIM))
    return _mix_ffn(x1, ht_f, ht_b, ot_l, y_pool, mod_all, npre, npost, hnorm,
                    w_out[0].astype(BF16), v1a, v1b, v2, tm=tm)
```

```python
import functools

import jax
import jax.numpy as jnp
from jax import lax
from jax.experimental import pallas as pl
from jax.experimental.pallas import tpu as pltpu

F32 = jnp.float32
BF16 = jnp.bfloat16

EPS = 1e-6
N_MOD = 9
HEADS = 4
HEAD_DIM = 128
M_W = HEADS * HEAD_DIM
POOL_WINDOWS = (2, 4, 8, 16)
POOL_GW = 128
POOL_W = POOL_GW * len(POOL_WINDOWS)
GRID_W = 64
POOL_PAD = 8
CHUNK = 256
GATE_COLS = 4 * HEADS
FF_CHUNK = 256
STATE_ROWS = HEAD_DIM + 16
MLSTM_CHUNKS_PER_STEP = 4
V7X_VMEM_LIMIT = 56 * 1024 * 1024

_NT = (((1,), (1,)), ((), ()))
_TN = (((0,), (0,)), ((), ()))


def _sigmoid(x):
    return 1.0 / (1.0 + jnp.exp(-x))


def _log_sigmoid(x):
    return jnp.minimum(x, 0.0) - jnp.log(1.0 + jnp.exp(-jnp.abs(x)))


def _unit_rms(x):
    return x * lax.rsqrt(jnp.mean(x * x, axis=-1, keepdims=True) + EPS)


def _ada_in(x, gain, scale, shift):
    return _unit_rms(x) * (gain * (1.0 + scale)) + shift


def _ada_out(x, z, gain, gate):
    return x + _unit_rms(z) * (gain * gate)


def _dot(a, b, dims=None):
    if dims is None:
        return jnp.dot(a, b, preferred_element_type=F32)
    return lax.dot_general(a, b, dims, preferred_element_type=F32)


def _const_spec(shape):
    nd = len(shape)
    return pl.BlockSpec(shape, lambda *_: (0,) * nd, pipeline_mode=pl.Buffered(1))


def _layer_spec(stacked, k):
    nd = stacked.ndim - 1
    return pl.BlockSpec((None,) + stacked.shape[1:], lambda *_: (k,) + (0,) * nd,
                        pipeline_mode=pl.Buffered(1))


def _params(sem):
    return pltpu.CompilerParams(dimension_semantics=sem, vmem_limit_bytes=V7X_VMEM_LIMIT)


def _mod_kernel(s_ref, w_ref, b_ref, o_ref):
    s = s_ref[...]
    act = (s * _sigmoid(s)).astype(BF16)
    o_ref[...] = _dot(act, w_ref[...].astype(BF16)) + b_ref[...]


def _modulation(s_rows, w_mod, b_mod):
    rows, d = s_rows.shape
    n = w_mod.shape[1]
    tn = 1024
    return pl.pallas_call(
        _mod_kernel,
        grid=(n // tn,),
        in_specs=[pl.BlockSpec((rows, d), lambda j: (0, 0)),
                  pl.BlockSpec((d, tn), lambda j: (0, j)),
                  pl.BlockSpec((1, tn), lambda j: (0, j))],
        out_specs=pl.BlockSpec((rows, tn), lambda j: (0, j)),
        out_shape=jax.ShapeDtypeStruct((rows, n), F32),
        compiler_params=_params(("arbitrary",)),
        name="mod",
    )(s_rows, w_mod, b_mod.reshape(1, n))


def _swiglu_into(acc_ref, y_ref, w1_ref, w2_ref):
    y = y_ref[...]
    ff = w2_ref.shape[0]
    for j in range(ff // FF_CHUNK):
        cols = slice(j * FF_CHUNK, (j + 1) * FF_CHUNK)
        a = _dot(y, w1_ref[:, cols])
        b = _dot(y, w1_ref[:, ff + j * FF_CHUNK:ff + (j + 1) * FF_CHUNK])
        h = ((a * _sigmoid(a)) * b).astype(BF16)
        part = _dot(h, w2_ref[cols, :])
        if j == 0:
            acc_ref[...] = part
        else:
            acc_ref[...] += part


def _ffn_in_kernel(with_rest, x_ref, mod_ref, npre_ref, npost_ref, w1_ref, w2_ref,
                   win_ref, wint_ref, gbias_ref, *rest):
    if with_rest:
        x1_ref, qk_ref, gt_ref, vt_ref, pool_ref, ot_ref, y_scr, acc_scr = rest
    else:
        qk_ref, gt_ref, vt_ref, y_scr, acc_scr = rest
    tm = x_ref.shape[0]
    x = x_ref[...]
    y = _ada_in(x, npre_ref[0:1, :], mod_ref[0, 1:2, :], mod_ref[0, 0:1, :])
    y_scr[...] = y.astype(BF16)
    _swiglu_into(acc_scr, y_scr, w1_ref, w2_ref)
    x1 = _ada_out(x, acc_scr[...], npost_ref[0:1, :], 0.5 * mod_ref[0, 2:3, :])
    y2 = _ada_in(x1, npre_ref[1:2, :], mod_ref[0, 4:5, :], mod_ref[0, 3:4, :])
    y_scr[...] = y2.astype(BF16)
    y2b = y_scr[...]
    qk_ref[...] = _dot(y2b, win_ref[:, 0:M_W])
    feat = _dot(wint_ref[...], y2b, _NT)
    g0, g1 = M_W, M_W + GATE_COLS
    gt_ref[...] = feat[g0:g1, :] + jnp.concatenate([gbias_ref[...]] * (tm // 128), axis=1)
    for c in range(vt_ref.shape[0]):
        vt_ref[c] = feat[0:g0, c * CHUNK:(c + 1) * CHUNK].astype(BF16)
    if with_rest:
        x1_ref[...] = x1
        pool_ref[...] = _dot(y2b, win_ref[:, M_W:M_W + POOL_W])
        for c in range(ot_ref.shape[0]):
            ot_ref[c] = feat[g1:g1 + M_W, c * CHUNK:(c + 1) * CHUNK]


def _ffn_in(x, mod_all, mod_row0, npre, npost, w1s, w2s, win, wint, gbias, *, with_rest, tm):
    bsz, n, d = x.shape
    grid = (bsz, n // tm)
    tok = lambda w: pl.BlockSpec((None, tm, w), lambda b, i: (b, i, 0))
    feat = pl.BlockSpec((None, tm // CHUNK, M_W, CHUNK), lambda b, i: (b, i, 0, 0))
    gates = pl.BlockSpec((None, GATE_COLS, tm), lambda b, i: (b, 0, i))
    if mod_row0 is None:
        mod_spec = pl.BlockSpec((1, N_MOD, d), lambda b, i: (b, 0, 0))
    else:
        mod_spec = pl.BlockSpec((1, N_MOD, d), lambda b, i: (mod_row0, 0, 0))
    in_specs = [tok(d), mod_spec, _const_spec(npre.shape), _const_spec(npost.shape),
                _layer_spec(w1s, 0), _layer_spec(w2s, 0),
                _const_spec(win.shape), _const_spec(wint.shape), _const_spec(gbias.shape)]
    sds = lambda w, dt: jax.ShapeDtypeStruct((bsz, n, w), dt)
    fsds = lambda dt: jax.ShapeDtypeStruct((bsz, n // CHUNK, M_W, CHUNK), dt)
    out_specs = [tok(M_W), gates, feat]
    out_shape = [sds(M_W, F32), jax.ShapeDtypeStruct((bsz, GATE_COLS, n), F32), fsds(BF16)]
    if with_rest:
        out_specs = [tok(d)] + out_specs + [tok(POOL_W), feat]
        out_shape = [sds(d, F32)] + out_shape + [sds(POOL_W, F32), fsds(F32)]
    return pl.pallas_call(
        functools.partial(_ffn_in_kernel, with_rest),
        grid=grid,
        in_specs=in_specs,
        out_specs=out_specs,
        out_shape=out_shape,
        scratch_shapes=[pltpu.VMEM((tm, d), BF16), pltpu.VMEM((tm, d), F32)],
        compiler_params=_params(("parallel", "arbitrary")),
        name="ffn_in" if with_rest else "ffn_in_ctx",
    )(x, mod_all, npre, npost, w1s, w2s, win, wint, gbias)


def _qk_kernel(nt, x_ref, prev_ref, next_ref, cw_ref, cb_ref, wqt_ref, wkt_ref, qt_ref, kt_ref):
    j = pl.program_id(1)
    x = x_ref[...]
    tq = x.shape[0]
    prev_row = prev_ref[7:8, :] * jnp.where(j > 0, 1.0, 0.0)
    next_row = next_ref[0:1, :] * jnp.where(j < nt - 1, 1.0, 0.0)
    row = lax.broadcasted_iota(jnp.int32, x.shape, 0)
    up = jnp.where(row == 0, prev_row, pltpu.roll(x, 1, 0))
    dn = jnp.where(row == tq - 1, next_row, pltpu.roll(x, tq - 1, 0))
    conv = cb_ref[...] + up * cw_ref[0:1, :] + x * cw_ref[1:2, :] + dn * cw_ref[2:3, :]
    u = (conv * _sigmoid(conv)).astype(BF16)
    scale = HEAD_DIM ** -0.5
    for h in range(HEADS):
        sl = slice(h * HEAD_DIM, (h + 1) * HEAD_DIM)
        uh = u[:, sl]
        qt = _dot(wqt_ref[h], uh, _NT).astype(BF16)
        kt = (_dot(wkt_ref[h], uh, _NT) * scale).astype(BF16)
        for c in range(qt_ref.shape[0]):
            qt_ref[c, sl, :] = qt[:, c * CHUNK:(c + 1) * CHUNK]
            kt_ref[c, sl, :] = kt[:, c * CHUNK:(c + 1) * CHUNK]


def _qk_proj(qk_src, conv_w, conv_b, wqt, wkt, *, tq):
    bsz, n, w = qk_src.shape
    nt = n // tq
    blk8 = tq // 8
    last8 = n // 8 - 1
    tok = pl.BlockSpec((None, tq, w), lambda b, j: (b, j, 0))
    prev = pl.BlockSpec((None, 8, w), lambda b, j: (b, jnp.maximum(j * blk8 - 1, 0), 0))
    nxt = pl.BlockSpec((None, 8, w), lambda b, j: (b, jnp.minimum((j + 1) * blk8, last8), 0))
    feat = pl.BlockSpec((None, tq // CHUNK, w, CHUNK), lambda b, j: (b, j, 0, 0))
    out = jax.ShapeDtypeStruct((bsz, n // CHUNK, w, CHUNK), BF16)
    return pl.pallas_call(
        functools.partial(_qk_kernel, nt),
        grid=(bsz, nt),
        in_specs=[tok, prev, nxt, _const_spec(conv_w.shape), _const_spec(conv_b.shape),
                  _const_spec(wqt.shape), _const_spec(wkt.shape)],
        out_specs=[feat, feat],
        out_shape=[out, out],
        compiler_params=_params(("parallel", "arbitrary")),
        name="qk",
    )(qk_src, qk_src, qk_src, conv_w, conv_b, wqt, wkt)


def _pool_kernel(nt, grid_h, x_ref, prev_ref, next_ref, wp_ref, ps_ref, y_ref, xs_ref, *bufs):
    j = pl.program_id(1)
    tt = x_ref.shape[0]
    halo = prev_ref.shape[0]
    halo_rows = halo // GRID_W
    rows = tt // GRID_W
    d0, d1 = POOL_PAD, POOL_PAD + GRID_W
    xs_ref[0:halo, :] = prev_ref[...] * jnp.where(j > 0, 1.0, 0.0)
    xs_ref[halo:halo + tt, :] = x_ref[...]
    xs_ref[halo + tt:halo + tt + halo, :] = next_ref[...] * jnp.where(j < nt - 1, 1.0, 0.0)
    zpad = jnp.zeros((rows, POOL_PAD, POOL_GW), F32)
    for buf in bufs:
        buf[:, 0:d0, :] = zpad
        buf[:, d1:d1 + POOL_PAD, :] = zpad
    c = lax.broadcasted_iota(jnp.int32, (1, GRID_W, POOL_GW), 1)
    r = j * rows + lax.broadcasted_iota(jnp.int32, (rows, 1, POOL_GW), 0)

    def shifted_sum(src, dst, k):
        dst[:, d0:d1, :] = src[:, d0:d1, :] + src[:, d0 + k:d1 + k, :]

    for g, win in enumerate(POOL_WINDOWS):
        lo, hi = win // 2, win - 1 - win // 2
        sl = slice(g * POOL_GW, (g + 1) * POOL_GW)
        first = (halo_rows - lo) * GRID_W
        vs = xs_ref[first:first + (rows + win - 1) * GRID_W, sl]
        span = 1
        while span < win:
            vs = vs[:-span * GRID_W] + vs[span * GRID_W:]
            span *= 2
        a, b, cc, dd = bufs
        a[:, d0:d1, :] = vs.reshape(rows, GRID_W, POOL_GW)
        trail, lead, spare_t, spare_l, k = a, a, b, dd, 1
        while k < lo:
            shifted_sum(trail, spare_t, -k)
            shifted_sum(lead, spare_l, k)
            trail, spare_t = spare_t, (cc if spare_t is b else b)
            lead, spare_l = spare_l, (a if spare_l is dd else dd)
            k *= 2
        tot = trail[:, d0 - 1:d1 - 1, :] + lead[:, d0:d1, :]
        cnt_c = jnp.minimum(c + hi, GRID_W - 1) - jnp.maximum(c - lo, 0) + 1
        cnt_r = jnp.minimum(r + hi, grid_h - 1) - jnp.maximum(r - lo, 0) + 1
        pg = tot * (1.0 / cnt_c.astype(F32)) * (1.0 / cnt_r.astype(F32))
        dlt = (pg.reshape(tt, POOL_GW) - x_ref[:, sl]).astype(BF16)
        y_ref[:, sl] = (_dot(dlt, wp_ref[g]) * ps_ref[:, sl]).astype(BF16)


def _pool_mixer(u, wp, ps, *, tt):
    bsz, n, w = u.shape
    nt = n // tt
    halo = 8 * GRID_W
    hb = tt // halo
    last = n // halo - 1
    tok = pl.BlockSpec((None, tt, w), lambda b, j: (b, j, 0))
    prev = pl.BlockSpec((None, halo, w), lambda b, j: (b, jnp.maximum(j * hb - 1, 0), 0))
    nxt = pl.BlockSpec((None, halo, w), lambda b, j: (b, jnp.minimum((j + 1) * hb, last), 0))
    return pl.pallas_call(
        functools.partial(_pool_kernel, nt, n // GRID_W),
        grid=(bsz, nt),
        in_specs=[tok, prev, nxt, _const_spec(wp.shape), _const_spec(ps.shape)],
        out_specs=tok,
        out_shape=jax.ShapeDtypeStruct((bsz, n, w), BF16),
        scratch_shapes=[pltpu.VMEM((tt + 2 * halo, w), F32)]
        + [pltpu.VMEM((tt // GRID_W, GRID_W + 2 * POOL_PAD, POOL_GW), F32)] * 4,
        compiler_params=_params(("parallel", "arbitrary")),
        name="pool",
    )(u, u, u, wp, ps)


def _split3(x):
    hi = x.astype(BF16).astype(F32)
    r = x - hi
    mid = r.astype(BF16).astype(F32)
    lo = (r - mid).astype(BF16).astype(F32)
    return hi, mid, lo


def _gate_scan_kernel(chunk, gt_ref, o_ref):
    n_combo = 2 * HEADS
    g = gt_ref[...]
    n = g.shape[1]
    pos = jnp.bitwise_and(lax.broadcasted_iota(jnp.int32, (n_combo, n), 1), chunk - 1)
    is_fwd_row = lax.broadcasted_iota(jnp.int32, (n_combo, n), 0) < HEADS

    def scan(x, op, ident):
        sft = 1
        while sft < chunk:
            from_l = jnp.where(pos >= sft, pltpu.roll(x, sft, 1), ident)
            from_r = jnp.where(pos < chunk - sft, pltpu.roll(x, n - sft, 1), ident)
            x = op(x, jnp.where(is_fwd_row, from_l, from_r))
            sft *= 2
        return x

    brow = scan(_log_sigmoid(g[n_combo:]), jnp.add, 0.0)
    rowb = g[:n_combo] - brow
    o_ref[0:n_combo, :] = rowb
    o_ref[n_combo:2 * n_combo, :] = brow
    o_ref[2 * n_combo:, :] = scan(rowb, jnp.maximum, -jnp.inf)


def _gate_scan(gt):
    bsz, rows, n = gt.shape
    return pl.pallas_call(
        functools.partial(_gate_scan_kernel, CHUNK),
        grid=(bsz,),
        in_specs=[pl.BlockSpec((None, rows, n), lambda b: (b, 0, 0))],
        out_specs=pl.BlockSpec((None, 3 * rows // 2, n), lambda b: (b, 0, 0)),
        out_shape=jax.ShapeDtypeStruct((bsz, 3 * rows // 2, n), F32),
        compiler_params=_params(("parallel",)),
        name="gate_scan",
    )(gt)


def _mlstm_kernel(emit_h, emit_state, *refs):
    fwd, bwd = refs[0:4], refs[4:8]
    ct0_ref, m0_ref = refs[8:10]
    outs = list(refs[10:])
    hf_ref, hb_ref = (outs.pop(0), outs.pop(0)) if emit_h else (None, None)
    cto_ref, mo_ref = (outs.pop(0), outs.pop(0)) if emit_state else (None, None)
    ct_scr, m_scr = outs
    i = pl.program_id(0)
    n_combo = 2 * HEADS
    bsz, cps, _, L = fwd[0].shape

    @pl.when(i == 0)
    def _():
        ct_scr[...] = ct0_ref[...]
        m_scr[...] = m0_ref[...]

    is_fwd_row = lax.broadcasted_iota(jnp.int32, (n_combo, L), 0) < HEADS
    is_fwd_col = lax.broadcasted_iota(jnp.int32, (n_combo, 1), 0) < HEADS
    krow = lax.broadcasted_iota(jnp.int32, (8, L), 0)
    tpos = lax.broadcasted_iota(jnp.int32, (L, L), 0)
    spos = lax.broadcasted_iota(jnp.int32, (L, L), 1)
    masks = (spos <= tpos, spos >= tpos)
    ones_rows = jnp.ones((STATE_ROWS - HEAD_DIM, L), BF16)

    def position_rows(b, slots):
        lanes = [slice(u * L, (u + 1) * L) for u in slots]
        pick = lambda r: jnp.where(is_fwd_row, fwd[3][b, r * n_combo:(r + 1) * n_combo, lanes[0]],
                                   bwd[3][b, r * n_combo:(r + 1) * n_combo, lanes[1]])
        last = lambda a: jnp.where(is_fwd_col, a[:, L - 1:L], a[:, 0:1])
        rowb, brow, cmax = pick(0), pick(1), pick(2)
        total, rowb_max = last(brow), last(cmax)
        m_in = m_scr[b][:, 0:1]
        mm = jnp.maximum(cmax, m_in)
        m_loc = total + rowb_max
        m_new = jnp.maximum(total + m_in, m_loc)
        m_scr[b] = jnp.broadcast_to(m_new, (n_combo, HEAD_DIM))
        return dict(w_inter=jnp.exp(m_in - mm), floor=jnp.exp(-(brow + mm)),
                    w_state=jnp.exp(rowb - rowb_max), a_old=jnp.exp(total + m_in - m_new),
                    a_new=jnp.exp(m_loc - m_new), rb3=_split3(rowb), nm3=_split3(-mm))

    def front(b, d, h, u, pr):
        c = d * HEADS + h
        sl = slice(h * HEAD_DIM, (h + 1) * HEAD_DIM)
        row = lambda a: a[c:c + 1, :]
        src = fwd if d == 0 else bwd
        qt, kt, vt = src[0][b, u, sl, :], src[1][b, u, sl, :], src[2][b, u, sl, :]
        vext = jnp.concatenate([vt, ones_rows], axis=0)
        ct = ct_scr[b * n_combo + c]
        out = dict(b=b, d=d, u=u, sl=sl, vext=vext, w_inter=row(pr["w_inter"]),
                   floor=row(pr["floor"]))
        if emit_h:
            nm3, rb3 = pr["nm3"], pr["rb3"]
            lhs = jnp.where(krow < 3, 1.0, jnp.where(krow == 3, row(nm3[0]), jnp.where(
                krow == 4, row(nm3[1]), jnp.where(krow == 5, row(nm3[2]), 0.0))))
            rhs = jnp.where(krow == 0, row(rb3[0]), jnp.where(krow == 1, row(rb3[1]), jnp.where(
                krow == 2, row(rb3[2]), jnp.where(krow < 6, 1.0, 0.0))))
            out["expo"] = _dot(lhs.astype(BF16), rhs.astype(BF16), _TN)
            out["s0"] = _dot(qt, kt, _TN)
            out["state_t"] = _dot(ct.astype(BF16), qt)
        ktw = (kt.astype(F32) * row(pr["w_state"])).astype(BF16)
        ckv = _dot(vext, ktw, _NT)
        ct_scr[b * n_combo + c] = pr["a_old"][c:c + 1, :] * ct + pr["a_new"][c:c + 1, :] * ckv
        return out

    def back(f):
        dmat = jnp.where(masks[f["d"]], jnp.exp(f["expo"]), 0.0)
        s = (f["s0"] * dmat).astype(BF16)
        res = f["w_inter"] * f["state_t"] + _dot(f["vext"], s, _NT)
        den = res[HEAD_DIM:HEAD_DIM + 1, :]
        h_ref = hf_ref if f["d"] == 0 else hb_ref
        h_ref[f["b"], f["u"], f["sl"], :] = res[:HEAD_DIM] / jnp.maximum(jnp.abs(den), f["floor"])

    pending = None
    for step in range(cps):
        slots = (step, cps - 1 - step)
        for b in range(bsz):
            pr = position_rows(b, slots)
            for d in range(2):
                for h in range(HEADS):
                    cur = front(b, d, h, slots[d], pr)
                    if emit_h and pending is not None:
                        back(pending)
                    pending = cur
    if emit_h:
        back(pending)

    if emit_state:
        @pl.when(i == pl.num_programs(0) - 1)
        def _():
            cto_ref[...] = ct_scr[...]
            mo_ref[...] = m_scr[...]


def _mlstm(qt, kt, vt, sc, ct0, m0, *, emit_h):
    bsz, nc, w, _ = qt.shape
    cps = MLSTM_CHUNKS_PER_STEP if nc % MLSTM_CHUNKS_PER_STEP == 0 else 1
    steps = nc // cps
    fpos = lambda i: i
    bpos = lambda i: steps - 1 - i

    def feat(pos_fn):
        return pl.BlockSpec((bsz, cps, w, CHUNK), lambda i: (0, pos_fn(i), 0, 0))

    def specs(pos_fn):
        scs = pl.BlockSpec((bsz, sc.shape[1], cps * CHUNK), lambda i: (0, 0, pos_fn(i)))
        return [feat(pos_fn)] * 3 + [scs]

    ct_spec = pl.BlockSpec(ct0.shape, lambda i: (0, 0, 0))
    m_spec = pl.BlockSpec(m0.shape, lambda i: (0, 0, 0))
    if emit_h:
        hsd = jax.ShapeDtypeStruct((bsz, nc, w, CHUNK), F32)
        out_specs = [feat(fpos), feat(bpos)]
        out_shape = [hsd, hsd]
    else:
        out_specs = [ct_spec, m_spec]
        out_shape = [jax.ShapeDtypeStruct(ct0.shape, F32), jax.ShapeDtypeStruct(m0.shape, F32)]
    return pl.pallas_call(
        functools.partial(_mlstm_kernel, emit_h, not emit_h),
        grid=(steps,),
        in_specs=specs(fpos) + specs(bpos) + [ct_spec, m_spec],
        out_specs=out_specs,
        out_shape=out_shape,
        scratch_shapes=[pltpu.VMEM(ct0.shape, F32), pltpu.VMEM(m0.shape, F32)],
        compiler_params=_params(("arbitrary",)),
        name="mlstm" if emit_h else "mlstm_ctx",
    )(qt, kt, vt, sc, qt, kt, vt, sc, ct0, m0)


def _mix_ffn_kernel(x1_ref, hf_ref, hb_ref, ot_ref, yp_ref, mod_ref, npre_ref, npost_ref, hn_ref,
                    wout_ref, w1_ref, w2_ref, out_ref, mixt_scr, y_scr, acc_scr):
    hnorm = jnp.concatenate([hn_ref[...]] * (CHUNK // HEAD_DIM), axis=1)
    for c in range(hf_ref.shape[0]):
        for h in range(HEADS):
            sl = slice(h * HEAD_DIM, (h + 1) * HEAD_DIM)
            hh = hf_ref[c, sl, :] + hb_ref[c, sl, :]
            inv = lax.rsqrt(jnp.mean(hh * hh, axis=0, keepdims=True) + EPS)
            gated = hh * inv * hnorm[sl, :] * _sigmoid(ot_ref[c, sl, :])
            mixt_scr[sl, c * CHUNK:(c + 1) * CHUNK] = gated.astype(BF16)
    z = _dot(mixt_scr[...], wout_ref[0:M_W, :], _TN) + _dot(yp_ref[...], wout_ref[M_W:, :])
    x2 = _ada_out(x1_ref[...], z, npost_ref[1:2, :], 1.0 * mod_ref[0, 5:6, :])
    y = _ada_in(x2, npre_ref[2:3, :], mod_ref[0, 7:8, :], mod_ref[0, 6:7, :])
    y_scr[...] = y.astype(BF16)
    _swiglu_into(acc_scr, y_scr, w1_ref, w2_ref)
    out_ref[...] = _ada_out(x2, acc_scr[...], npost_ref[2:3, :], 0.5 * mod_ref[0, 8:9, :])


def _mix_ffn(x1, hf, hb, ot, yp, mod_all, npre, npost, hnorm, wout, w1s, w2s, *, tm):
    bsz, n, d = x1.shape
    tok = lambda w: pl.BlockSpec((None, tm, w), lambda b, i: (b, i, 0))
    feat = pl.BlockSpec((None, tm // CHUNK, M_W, CHUNK), lambda b, i: (b, i, 0, 0))
    in_specs = [tok(d), feat, feat, feat, tok(POOL_W),
                pl.BlockSpec((1, N_MOD, d), lambda b, i: (b, 0, 0)),
                _const_spec(npre.shape), _const_spec(npost.shape), _const_spec(hnorm.shape),
                _const_spec(wout.shape), _layer_spec(w1s, 1), _layer_spec(w2s, 1)]
    return pl.pallas_call(
        _mix_ffn_kernel,
        grid=(bsz, n // tm),
        in_specs=in_specs,
        out_specs=tok(d),
        out_shape=jax.ShapeDtypeStruct((bsz, n, d), F32),
        scratch_shapes=[pltpu.VMEM((M_W, tm), BF16), pltpu.VMEM((tm, d), BF16),
                        pltpu.VMEM((tm, d), F32)],
        compiler_params=_params(("parallel", "arbitrary")),
        name="mix_ffn",
    )(x1, hf, hb, ot, yp, mod_all, npre, npost, hnorm, wout, w1s, w2s)


def kernel(x, c, ctx, c_ctx, w_mod, b_mod, norm_pre, norm_post, ffn_w_in, ffn_w_out, w_in, w_out,
           conv_w, conv_b, w_q, w_k, i_bias, f_bias, head_norm, pool_w, pool_scale):
    assert w_mod.shape[0] == 1, "single-layer stack"
    bsz, n, d = x.shape
    n_ctx = ctx.shape[1]
    m_cols = 3 * M_W + GATE_COLS

    rows = 8
    s_rows = jnp.zeros((rows, d), F32).at[:bsz].set(c).at[bsz].set(c_ctx)
    mod_all = _modulation(s_rows, w_mod[0], b_mod[0]).reshape(rows, N_MOD, d)

    w1s, w2s = ffn_w_in[0].astype(BF16), ffn_w_out[0].astype(BF16)
    wi = w_in[0]
    win_lat = jnp.concatenate([wi[:, :M_W], wi[:, m_cols:]], axis=1).astype(BF16)
    win_ctx = win_lat[:, :M_W]
    wint_lat = jnp.concatenate([wi[:, M_W:2 * M_W], wi[:, 3 * M_W:m_cols], wi[:, 2 * M_W:3 * M_W]],
                               axis=1).T.astype(BF16)
    wint_ctx = wint_lat[:M_W + GATE_COLS]
    gbias = jnp.broadcast_to(
        jnp.concatenate([i_bias[0].reshape(-1), f_bias[0].reshape(-1)]).reshape(GATE_COLS, 1),
        (GATE_COLS, 128))
    npre, npost = norm_pre[0], norm_post[0]

    tm = min(512, n)
    x1, qk_l, gt_l, vt_l, pool_l, ot_l = _ffn_in(x, mod_all, None, npre, npost, w1s, w2s,
                                                 win_lat, wint_lat, gbias, with_rest=True, tm=tm)
    assert n_ctx % CHUNK == 0
    qk_c, gt_c, vt_c = _ffn_in(ctx.reshape(1, bsz * n_ctx, d), mod_all, bsz, npre, npost, w1s, w2s,
                               win_ctx, wint_ctx, gbias, with_rest=False,
                               tm=min(512, bsz * n_ctx))
    qk_c = qk_c.reshape(bsz, n_ctx, M_W)
    vt_c = vt_c.reshape(bsz, n_ctx // CHUNK, M_W, CHUNK)
    gt_c = gt_c.reshape(GATE_COLS, bsz, n_ctx).transpose(1, 0, 2)

    cb = conv_b[0].reshape(1, M_W)
    wqt = jnp.swapaxes(w_q[0], 1, 2).astype(BF16)
    wkt = jnp.swapaxes(w_k[0], 1, 2).astype(BF16)
    qt_l, kt_l = _qk_proj(qk_l, conv_w[0], cb, wqt, wkt, tq=min(1024, n))
    qt_c, kt_c = _qk_proj(qk_c, conv_w[0], cb, wqt, wkt, tq=min(1024, n_ctx))

    y_pool = _pool_mixer(pool_l, pool_w[0].astype(BF16), pool_scale[0].reshape(1, POOL_W),
                         tt=min(4096, n))

    ct0 = jnp.zeros((bsz * 2 * HEADS, STATE_ROWS, HEAD_DIM), F32)
    m0 = jnp.zeros((bsz, 2 * HEADS, HEAD_DIM), F32)
    ct1, m1 = _mlstm(qt_c, kt_c, vt_c, _gate_scan(gt_c), ct0, m0, emit_h=False)
    ht_f, ht_b = _mlstm(qt_l, kt_l, vt_l, _gate_scan(gt_l), ct1, m1, emit_h=True)

    hnorm = jnp.broadcast_to(head_norm[0].reshape(M_W, 1), (M_W, HEAD_DIM))
    return _mix_ffn(x1, ht_f, ht_b, ot_l, y_pool, mod_all, npre, npost, hnorm,
                    w_out[0].astype(BF16), w1s, w2s, tm=tm)
```

```python
import functools

import jax
import jax.numpy as jnp
from jax import lax
from jax.experimental import pallas as pl
from jax.experimental.pallas import tpu as pltpu

F32 = jnp.float32
BF16 = jnp.bfloat16

EPS = 1e-6
N_MOD = 9
HEADS = 4
HEAD_DIM = 128
M_W = HEADS * HEAD_DIM
POOL_WINDOWS = (2, 4, 8, 16)
POOL_GW = 128
POOL_W = POOL_GW * len(POOL_WINDOWS)
GRID_W = 64
POOL_PAD = 8
CHUNK = 256
GATE_COLS = 4 * HEADS
FF_CHUNK = 256
STATE_ROWS = HEAD_DIM + 16
MLSTM_CHUNKS_PER_STEP = 4
V7X_VMEM_LIMIT = 56 * 1024 * 1024

_NT = (((1,), (1,)), ((), ()))
_TN = (((0,), (0,)), ((), ()))


def _sigmoid(x):
    return 1.0 / (1.0 + jnp.exp(-x))


def _log_sigmoid(x):
    return jnp.minimum(x, 0.0) - jnp.log(1.0 + jnp.exp(-jnp.abs(x)))


def _unit_rms(x):
    return x * lax.rsqrt(jnp.mean(x * x, axis=-1, keepdims=True) + EPS)


def _ada_in(x, gain, scale, shift):
    return _unit_rms(x) * (gain * (1.0 + scale)) + shift


def _ada_out(x, z, gain, gate):
    return x + _unit_rms(z) * (gain * gate)


def _dot(a, b, dims=None):
    if dims is None:
        return jnp.dot(a, b, preferred_element_type=F32)
    return lax.dot_general(a, b, dims, preferred_element_type=F32)


def _const_spec(shape):
    nd = len(shape)
    return pl.BlockSpec(shape, lambda *_: (0,) * nd, pipeline_mode=pl.Buffered(1))


def _layer_spec(stacked, k):
    nd = stacked.ndim - 1
    return pl.BlockSpec((None,) + stacked.shape[1:], lambda *_: (k,) + (0,) * nd,
                        pipeline_mode=pl.Buffered(1))


def _params(sem):
    return pltpu.CompilerParams(dimension_semantics=sem, vmem_limit_bytes=V7X_VMEM_LIMIT)


def _mod_kernel(s_ref, w_ref, b_ref, o_ref):
    s = s_ref[...]
    act = (s * _sigmoid(s)).astype(BF16)
    o_ref[...] = _dot(act, w_ref[...].astype(BF16)) + b_ref[...]


def _modulation(s_rows, w_mod, b_mod):
    rows, d = s_rows.shape
    n = w_mod.shape[1]
    tn = 1024
    return pl.pallas_call(
        _mod_kernel,
        grid=(n // tn,),
        in_specs=[pl.BlockSpec((rows, d), lambda j: (0, 0)),
                  pl.BlockSpec((d, tn), lambda j: (0, j)),
                  pl.BlockSpec((1, tn), lambda j: (0, j))],
        out_specs=pl.BlockSpec((rows, tn), lambda j: (0, j)),
        out_shape=jax.ShapeDtypeStruct((rows, n), F32),
        compiler_params=_params(("arbitrary",)),
        name="mod",
    )(s_rows, w_mod, b_mod.reshape(1, n))


def _swiglu_into(acc_ref, y_ref, w1_ref, w2_ref):
    y = y_ref[...]
    ff = w2_ref.shape[0]
    for j in range(ff // FF_CHUNK):
        cols = slice(j * FF_CHUNK, (j + 1) * FF_CHUNK)
        a = _dot(y, w1_ref[:, cols])
        b = _dot(y, w1_ref[:, ff + j * FF_CHUNK:ff + (j + 1) * FF_CHUNK])
        h = ((a * _sigmoid(a)) * b).astype(BF16)
        part = _dot(h, w2_ref[cols, :])
        if j == 0:
            acc_ref[...] = part
        else:
            acc_ref[...] += part


def _ffn_in_kernel(with_rest, x_ref, mod_ref, npre_ref, npost_ref, w1_ref, w2_ref,
                   win_ref, wint_ref, gbias_ref, *rest):
    if with_rest:
        x1_ref, qk_ref, gt_ref, vt_ref, pool_ref, ot_ref, y_scr, acc_scr = rest
    else:
        qk_ref, gt_ref, vt_ref, y_scr, acc_scr = rest
    tm = x_ref.shape[0]
    x = x_ref[...]
    y = _ada_in(x, npre_ref[0:1, :], mod_ref[0, 1:2, :], mod_ref[0, 0:1, :])
    y_scr[...] = y.astype(BF16)
    _swiglu_into(acc_scr, y_scr, w1_ref, w2_ref)
    x1 = _ada_out(x, acc_scr[...], npost_ref[0:1, :], 0.5 * mod_ref[0, 2:3, :])
    y2 = _ada_in(x1, npre_ref[1:2, :], mod_ref[0, 4:5, :], mod_ref[0, 3:4, :])
    y_scr[...] = y2.astype(BF16)
    y2b = y_scr[...]
    qk_ref[...] = _dot(y2b, win_ref[:, 0:M_W])
    feat = _dot(wint_ref[...], y2b, _NT)
    g0, g1 = M_W, M_W + GATE_COLS
    gt_ref[...] = feat[g0:g1, :] + jnp.concatenate([gbias_ref[...]] * (tm // 128), axis=1)
    for c in range(vt_ref.shape[0]):
        vt_ref[c] = feat[0:g0, c * CHUNK:(c + 1) * CHUNK].astype(BF16)
    if with_rest:
        x1_ref[...] = x1
        pool_ref[...] = _dot(y2b, win_ref[:, M_W:M_W + POOL_W])
        for c in range(ot_ref.shape[0]):
            ot_ref[c] = feat[g1:g1 + M_W, c * CHUNK:(c + 1) * CHUNK].astype(BF16)


def _ffn_in(x, mod_all, mod_row0, npre, npost, w1s, w2s, win, wint, gbias, *, with_rest, tm):
    bsz, n, d = x.shape
    grid = (bsz, n // tm)
    tok = lambda w: pl.BlockSpec((None, tm, w), lambda b, i: (b, i, 0))
    feat = pl.BlockSpec((None, tm // CHUNK, M_W, CHUNK), lambda b, i: (b, i, 0, 0))
    gates = pl.BlockSpec((None, GATE_COLS, tm), lambda b, i: (b, 0, i))
    if mod_row0 is None:
        mod_spec = pl.BlockSpec((1, N_MOD, d), lambda b, i: (b, 0, 0))
    else:
        mod_spec = pl.BlockSpec((1, N_MOD, d), lambda b, i: (mod_row0, 0, 0))
    in_specs = [tok(d), mod_spec, _const_spec(npre.shape), _const_spec(npost.shape),
                _layer_spec(w1s, 0), _layer_spec(w2s, 0),
                _const_spec(win.shape), _const_spec(wint.shape), _const_spec(gbias.shape)]
    sds = lambda w, dt: jax.ShapeDtypeStruct((bsz, n, w), dt)
    fsds = lambda dt: jax.ShapeDtypeStruct((bsz, n // CHUNK, M_W, CHUNK), dt)
    out_specs = [tok(M_W), gates, feat]
    out_shape = [sds(M_W, F32), jax.ShapeDtypeStruct((bsz, GATE_COLS, n), F32), fsds(BF16)]
    if with_rest:
        out_specs = [tok(d)] + out_specs + [tok(POOL_W), feat]
        out_shape = [sds(d, F32)] + out_shape + [sds(POOL_W, F32), fsds(BF16)]
    return pl.pallas_call(
        functools.partial(_ffn_in_kernel, with_rest),
        grid=grid,
        in_specs=in_specs,
        out_specs=out_specs,
        out_shape=out_shape,
        scratch_shapes=[pltpu.VMEM((tm, d), BF16), pltpu.VMEM((tm, d), F32)],
        compiler_params=_params(("parallel", "arbitrary")),
        name="ffn_in" if with_rest else "ffn_in_ctx",
    )(x, mod_all, npre, npost, w1s, w2s, win, wint, gbias)


def _qk_kernel(nt, x_ref, prev_ref, next_ref, cw_ref, cb_ref, wqt_ref, wkt_ref, qt_ref, kt_ref):
    j = pl.program_id(1)
    x = x_ref[...]
    tq = x.shape[0]
    prev_row = prev_ref[7:8, :] * jnp.where(j > 0, 1.0, 0.0)
    next_row = next_ref[0:1, :] * jnp.where(j < nt - 1, 1.0, 0.0)
    row = lax.broadcasted_iota(jnp.int32, x.shape, 0)
    up = jnp.where(row == 0, prev_row, pltpu.roll(x, 1, 0))
    dn = jnp.where(row == tq - 1, next_row, pltpu.roll(x, tq - 1, 0))
    conv = cb_ref[...] + up * cw_ref[0:1, :] + x * cw_ref[1:2, :] + dn * cw_ref[2:3, :]
    u = (conv * _sigmoid(conv)).astype(BF16)
    scale = HEAD_DIM ** -0.5
    for h in range(HEADS):
        sl = slice(h * HEAD_DIM, (h + 1) * HEAD_DIM)
        uh = u[:, sl]
        qt = _dot(wqt_ref[h], uh, _NT).astype(BF16)
        kt = (_dot(wkt_ref[h], uh, _NT) * scale).astype(BF16)
        for c in range(qt_ref.shape[0]):
            qt_ref[c, sl, :] = qt[:, c * CHUNK:(c + 1) * CHUNK]
            kt_ref[c, sl, :] = kt[:, c * CHUNK:(c + 1) * CHUNK]


def _qk_proj(qk_src, conv_w, conv_b, wqt, wkt, *, tq):
    bsz, n, w = qk_src.shape
    nt = n // tq
    blk8 = tq // 8
    last8 = n // 8 - 1
    tok = pl.BlockSpec((None, tq, w), lambda b, j: (b, j, 0))
    prev = pl.BlockSpec((None, 8, w), lambda b, j: (b, jnp.maximum(j * blk8 - 1, 0), 0))
    nxt = pl.BlockSpec((None, 8, w), lambda b, j: (b, jnp.minimum((j + 1) * blk8, last8), 0))
    feat = pl.BlockSpec((None, tq // CHUNK, w, CHUNK), lambda b, j: (b, j, 0, 0))
    out = jax.ShapeDtypeStruct((bsz, n // CHUNK, w, CHUNK), BF16)
    return pl.pallas_call(
        functools.partial(_qk_kernel, nt),
        grid=(bsz, nt),
        in_specs=[tok, prev, nxt, _const_spec(conv_w.shape), _const_spec(conv_b.shape),
                  _const_spec(wqt.shape), _const_spec(wkt.shape)],
        out_specs=[feat, feat],
        out_shape=[out, out],
        compiler_params=_params(("parallel", "arbitrary")),
        name="qk",
    )(qk_src, qk_src, qk_src, conv_w, conv_b, wqt, wkt)


def _pool_kernel(nt, grid_h, x_ref, prev_ref, next_ref, wp_ref, ps_ref, y_ref, xs_ref, *bufs):
    j = pl.program_id(1)
    tt = x_ref.shape[0]
    halo = prev_ref.shape[0]
    halo_rows = halo // GRID_W
    rows = tt // GRID_W
    d0, d1 = POOL_PAD, POOL_PAD + GRID_W
    xs_ref[0:halo, :] = prev_ref[...] * jnp.where(j > 0, 1.0, 0.0)
    xs_ref[halo:halo + tt, :] = x_ref[...]
    xs_ref[halo + tt:halo + tt + halo, :] = next_ref[...] * jnp.where(j < nt - 1, 1.0, 0.0)
    zpad = jnp.zeros((rows, POOL_PAD, POOL_GW), F32)
    for buf in bufs:
        buf[:, 0:d0, :] = zpad
        buf[:, d1:d1 + POOL_PAD, :] = zpad
    c = lax.broadcasted_iota(jnp.int32, (1, GRID_W, POOL_GW), 1)
    r = j * rows + lax.broadcasted_iota(jnp.int32, (rows, 1, POOL_GW), 0)

    def shifted_sum(src, dst, k):
        dst[:, d0:d1, :] = src[:, d0:d1, :] + src[:, d0 + k:d1 + k, :]

    for g, win in enumerate(POOL_WINDOWS):
        lo, hi = win // 2, win - 1 - win // 2
        sl = slice(g * POOL_GW, (g + 1) * POOL_GW)
        first = (halo_rows - lo) * GRID_W
        vs = xs_ref[first:first + (rows + win - 1) * GRID_W, sl]
        span = 1
        while span < win:
            vs = vs[:-span * GRID_W] + vs[span * GRID_W:]
            span *= 2
        a, b, cc, dd = bufs
        a[:, d0:d1, :] = vs.reshape(rows, GRID_W, POOL_GW)
        trail, lead, spare_t, spare_l, k = a, a, b, dd, 1
        while k < lo:
            shifted_sum(trail, spare_t, -k)
            shifted_sum(lead, spare_l, k)
            trail, spare_t = spare_t, (cc if spare_t is b else b)
            lead, spare_l = spare_l, (a if spare_l is dd else dd)
            k *= 2
        tot = trail[:, d0 - 1:d1 - 1, :] + lead[:, d0:d1, :]
        cnt_c = jnp.minimum(c + hi, GRID_W - 1) - jnp.maximum(c - lo, 0) + 1
        cnt_r = jnp.minimum(r + hi, grid_h - 1) - jnp.maximum(r - lo, 0) + 1
        pg = tot * (1.0 / cnt_c.astype(F32)) * (1.0 / cnt_r.astype(F32))
        dlt = (pg.reshape(tt, POOL_GW) - x_ref[:, sl]).astype(BF16)
        y_ref[:, sl] = (_dot(dlt, wp_ref[g]) * ps_ref[:, sl]).astype(BF16)


def _pool_mixer(u, wp, ps, *, tt):
    bsz, n, w = u.shape
    nt = n // tt
    halo = 8 * GRID_W
    hb = tt // halo
    last = n // halo - 1
    tok = pl.BlockSpec((None, tt, w), lambda b, j: (b, j, 0))
    prev = pl.BlockSpec((None, halo, w), lambda b, j: (b, jnp.maximum(j * hb - 1, 0), 0))
    nxt = pl.BlockSpec((None, halo, w), lambda b, j: (b, jnp.minimum((j + 1) * hb, last), 0))
    return pl.pallas_call(
        functools.partial(_pool_kernel, nt, n // GRID_W),
        grid=(bsz, nt),
        in_specs=[tok, prev, nxt, _const_spec(wp.shape), _const_spec(ps.shape)],
        out_specs=tok,
        out_shape=jax.ShapeDtypeStruct((bsz, n, w), BF16),
        scratch_shapes=[pltpu.VMEM((tt + 2 * halo, w), F32)]
        + [pltpu.VMEM((tt // GRID_W, GRID_W + 2 * POOL_PAD, POOL_GW), F32)] * 4,
        compiler_params=_params(("parallel", "arbitrary")),
        name="pool",
    )(u, u, u, wp, ps)


def _split3(x):
    hi = x.astype(BF16).astype(F32)
    r = x - hi
    mid = r.astype(BF16).astype(F32)
    lo = (r - mid).astype(BF16).astype(F32)
    return hi, mid, lo


def _gate_scan_kernel(chunk, gt_ref, o_ref):
    n_combo = 2 * HEADS
    g = gt_ref[...]
    n = g.shape[1]
    pos = jnp.bitwise_and(lax.broadcasted_iota(jnp.int32, (n_combo, n), 1), chunk - 1)
    is_fwd_row = lax.broadcasted_iota(jnp.int32, (n_combo, n), 0) < HEADS

    def scan(x, op, ident):
        sft = 1
        while sft < chunk:
            from_l = jnp.where(pos >= sft, pltpu.roll(x, sft, 1), ident)
            from_r = jnp.where(pos < chunk - sft, pltpu.roll(x, n - sft, 1), ident)
            x = op(x, jnp.where(is_fwd_row, from_l, from_r))
            sft *= 2
        return x

    brow = scan(_log_sigmoid(g[n_combo:]), jnp.add, 0.0)
    rowb = g[:n_combo] - brow
    o_ref[0:n_combo, :] = rowb
    o_ref[n_combo:2 * n_combo, :] = brow
    o_ref[2 * n_combo:, :] = scan(rowb, jnp.maximum, -jnp.inf)


def _gate_scan(gt):
    bsz, rows, n = gt.shape
    return pl.pallas_call(
        functools.partial(_gate_scan_kernel, CHUNK),
        grid=(bsz,),
        in_specs=[pl.BlockSpec((None, rows, n), lambda b: (b, 0, 0))],
        out_specs=pl.BlockSpec((None, 3 * rows // 2, n), lambda b: (b, 0, 0)),
        out_shape=jax.ShapeDtypeStruct((bsz, 3 * rows // 2, n), F32),
        compiler_params=_params(("parallel",)),
        name="gate_scan",
    )(gt)


def _mlstm_kernel(emit_h, emit_state, *refs):
    fwd, bwd = refs[0:4], refs[4:8]
    ct0_ref, m0_ref = refs[8:10]
    outs = list(refs[10:])
    hf_ref, hb_ref = (outs.pop(0), outs.pop(0)) if emit_h else (None, None)
    cto_ref, mo_ref = (outs.pop(0), outs.pop(0)) if emit_state else (None, None)
    ct_scr, m_scr = outs
    i = pl.program_id(0)
    n_combo = 2 * HEADS
    bsz, cps, _, L = fwd[0].shape

    @pl.when(i == 0)
    def _():
        ct_scr[...] = ct0_ref[...]
        m_scr[...] = m0_ref[...]

    is_fwd_row = lax.broadcasted_iota(jnp.int32, (n_combo, L), 0) < HEADS
    is_fwd_col = lax.broadcasted_iota(jnp.int32, (n_combo, 1), 0) < HEADS
    krow = lax.broadcasted_iota(jnp.int32, (8, L), 0)
    tpos = lax.broadcasted_iota(jnp.int32, (L, L), 0)
    spos = lax.broadcasted_iota(jnp.int32, (L, L), 1)
    masks = (spos <= tpos, spos >= tpos)
    ones_rows = jnp.ones((STATE_ROWS - HEAD_DIM, L), BF16)

    def position_rows(b, slots):
        lanes = [slice(u * L, (u + 1) * L) for u in slots]
        pick = lambda r: jnp.where(is_fwd_row, fwd[3][b, r * n_combo:(r + 1) * n_combo, lanes[0]],
                                   bwd[3][b, r * n_combo:(r + 1) * n_combo, lanes[1]])
        last = lambda a: jnp.where(is_fwd_col, a[:, L - 1:L], a[:, 0:1])
        rowb, brow, cmax = pick(0), pick(1), pick(2)
        total, rowb_max = last(brow), last(cmax)
        m_in = m_scr[b][:, 0:1]
        mm = jnp.maximum(cmax, m_in)
        m_loc = total + rowb_max
        m_new = jnp.maximum(total + m_in, m_loc)
        m_scr[b] = jnp.broadcast_to(m_new, (n_combo, HEAD_DIM))
        return dict(w_inter=jnp.exp(m_in - mm), floor=jnp.exp(-(brow + mm)),
                    w_state=jnp.exp(rowb - rowb_max), a_old=jnp.exp(total + m_in - m_new),
                    a_new=jnp.exp(m_loc - m_new), rb3=_split3(rowb), nm3=_split3(-mm))

    def front(b, d, h, u, pr):
        c = d * HEADS + h
        sl = slice(h * HEAD_DIM, (h + 1) * HEAD_DIM)
        row = lambda a: a[c:c + 1, :]
        src = fwd if d == 0 else bwd
        qt, kt, vt = src[0][b, u, sl, :], src[1][b, u, sl, :], src[2][b, u, sl, :]
        vext = jnp.concatenate([vt, ones_rows], axis=0)
        ct = ct_scr[b * n_combo + c]
        out = dict(b=b, d=d, u=u, sl=sl, vext=vext, w_inter=row(pr["w_inter"]),
                   floor=row(pr["floor"]))
        if emit_h:
            nm3, rb3 = pr["nm3"], pr["rb3"]
            lhs = jnp.where(krow < 3, 1.0, jnp.where(krow == 3, row(nm3[0]), jnp.where(
                krow == 4, row(nm3[1]), jnp.where(krow == 5, row(nm3[2]), 0.0))))
            rhs = jnp.where(krow == 0, row(rb3[0]), jnp.where(krow == 1, row(rb3[1]), jnp.where(
                krow == 2, row(rb3[2]), jnp.where(krow < 6, 1.0, 0.0))))
            out["expo"] = _dot(lhs.astype(BF16), rhs.astype(BF16), _TN)
            out["s0"] = _dot(qt, kt, _TN)
            out["state_t"] = _dot(ct.astype(BF16), qt)
        ktw = (kt.astype(F32) * row(pr["w_state"])).astype(BF16)
        ckv = _dot(vext, ktw, _NT)
        ct_scr[b * n_combo + c] = pr["a_old"][c:c + 1, :] * ct + pr["a_new"][c:c + 1, :] * ckv
        return out

    def back(f):
        dmat = jnp.where(masks[f["d"]], jnp.exp(f["expo"]), 0.0)
        s = (f["s0"] * dmat).astype(BF16)
        res = f["w_inter"] * f["state_t"] + _dot(f["vext"], s, _NT)
        den = res[HEAD_DIM:HEAD_DIM + 1, :]
        h_ref = hf_ref if f["d"] == 0 else hb_ref
        hval = res[:HEAD_DIM] / jnp.maximum(jnp.abs(den), f["floor"])
        h_ref[f["b"], f["u"], f["sl"], :] = hval.astype(h_ref.dtype)

    pending = None
    for step in range(cps):
        slots = (step, cps - 1 - step)
        for b in range(bsz):
            pr = position_rows(b, slots)
            for d in range(2):
                for h in range(HEADS):
                    cur = front(b, d, h, slots[d], pr)
                    if emit_h and pending is not None:
                        back(pending)
                    pending = cur
    if emit_h:
        back(pending)

    if emit_state:
        @pl.when(i == pl.num_programs(0) - 1)
        def _():
            cto_ref[...] = ct_scr[...]
            mo_ref[...] = m_scr[...]


def _mlstm(qt, kt, vt, sc, ct0, m0, *, emit_h):
    bsz, nc, w, _ = qt.shape
    cps = MLSTM_CHUNKS_PER_STEP if nc % MLSTM_CHUNKS_PER_STEP == 0 else 1
    steps = nc // cps
    fpos = lambda i: i
    bpos = lambda i: steps - 1 - i

    def feat(pos_fn):
        return pl.BlockSpec((bsz, cps, w, CHUNK), lambda i: (0, pos_fn(i), 0, 0))

    def specs(pos_fn):
        scs = pl.BlockSpec((bsz, sc.shape[1], cps * CHUNK), lambda i: (0, 0, pos_fn(i)))
        return [feat(pos_fn)] * 3 + [scs]

    ct_spec = pl.BlockSpec(ct0.shape, lambda i: (0, 0, 0))
    m_spec = pl.BlockSpec(m0.shape, lambda i: (0, 0, 0))
    if emit_h:
        hsd = jax.ShapeDtypeStruct((bsz, nc, w, CHUNK), BF16)
        out_specs = [feat(fpos), feat(bpos)]
        out_shape = [hsd, hsd]
    else:
        out_specs = [ct_spec, m_spec]
        out_shape = [jax.ShapeDtypeStruct(ct0.shape, F32), jax.ShapeDtypeStruct(m0.shape, F32)]
    return pl.pallas_call(
        functools.partial(_mlstm_kernel, emit_h, not emit_h),
        grid=(steps,),
        in_specs=specs(fpos) + specs(bpos) + [ct_spec, m_spec],
        out_specs=out_specs,
        out_shape=out_shape,
        scratch_shapes=[pltpu.VMEM(ct0.shape, F32), pltpu.VMEM(m0.shape, F32)],
        compiler_params=_params(("arbitrary",)),
        name="mlstm" if emit_h else "mlstm_ctx",
    )(qt, kt, vt, sc, qt, kt, vt, sc, ct0, m0)


def _mix_ffn_kernel(x1_ref, hf_ref, hb_ref, ot_ref, yp_ref, mod_ref, npre_ref, npost_ref, hn_ref,
                    wout_ref, w1_ref, w2_ref, out_ref, mixt_scr, y_scr, acc_scr):
    hnorm = jnp.concatenate([hn_ref[...]] * (CHUNK // HEAD_DIM), axis=1)
    for c in range(hf_ref.shape[0]):
        for h in range(HEADS):
            sl = slice(h * HEAD_DIM, (h + 1) * HEAD_DIM)
            hh = hf_ref[c, sl, :].astype(F32) + hb_ref[c, sl, :].astype(F32)
            inv = lax.rsqrt(jnp.mean(hh * hh, axis=0, keepdims=True) + EPS)
            gated = hh * inv * hnorm[sl, :] * _sigmoid(ot_ref[c, sl, :].astype(F32))
            mixt_scr[sl, c * CHUNK:(c + 1) * CHUNK] = gated.astype(BF16)
    z = _dot(mixt_scr[...], wout_ref[0:M_W, :], _TN) + _dot(yp_ref[...], wout_ref[M_W:, :])
    x2 = _ada_out(x1_ref[...], z, npost_ref[1:2, :], 1.0 * mod_ref[0, 5:6, :])
    y = _ada_in(x2, npre_ref[2:3, :], mod_ref[0, 7:8, :], mod_ref[0, 6:7, :])
    y_scr[...] = y.astype(BF16)
    _swiglu_into(acc_scr, y_scr, w1_ref, w2_ref)
    out_ref[...] = _ada_out(x2, acc_scr[...], npost_ref[2:3, :], 0.5 * mod_ref[0, 8:9, :])


def _mix_ffn(x1, hf, hb, ot, yp, mod_all, npre, npost, hnorm, wout, w1s, w2s, *, tm):
    bsz, n, d = x1.shape
    tok = lambda w: pl.BlockSpec((None, tm, w), lambda b, i: (b, i, 0))
    feat = pl.BlockSpec((None, tm // CHUNK, M_W, CHUNK), lambda b, i: (b, i, 0, 0))
    in_specs = [tok(d), feat, feat, feat, tok(POOL_W),
                pl.BlockSpec((1, N_MOD, d), lambda b, i: (b, 0, 0)),
                _const_spec(npre.shape), _const_spec(npost.shape), _const_spec(hnorm.shape),
                _const_spec(wout.shape), _layer_spec(w1s, 1), _layer_spec(w2s, 1)]
    return pl.pallas_call(
        _mix_ffn_kernel,
        grid=(bsz, n // tm),
        in_specs=in_specs,
        out_specs=tok(d),
        out_shape=jax.ShapeDtypeStruct((bsz, n, d), F32),
        scratch_shapes=[pltpu.VMEM((M_W, tm), BF16), pltpu.VMEM((tm, d), BF16),
                        pltpu.VMEM((tm, d), F32)],
        compiler_params=_params(("parallel", "arbitrary")),
        name="mix_ffn",
    )(x1, hf, hb, ot, yp, mod_all, npre, npost, hnorm, wout, w1s, w2s)


def kernel(x, c, ctx, c_ctx, w_mod, b_mod, norm_pre, norm_post, ffn_w_in, ffn_w_out, w_in, w_out,
           conv_w, conv_b, w_q, w_k, i_bias, f_bias, head_norm, pool_w, pool_scale):
    assert w_mod.shape[0] == 1, "single-layer stack"
    bsz, n, d = x.shape
    n_ctx = ctx.shape[1]
    m_cols = 3 * M_W + GATE_COLS

    rows = 8
    s_rows = jnp.zeros((rows, d), F32).at[:bsz].set(c).at[bsz].set(c_ctx)
    mod_all = _modulation(s_rows, w_mod[0], b_mod[0]).reshape(rows, N_MOD, d)

    w1s, w2s = ffn_w_in[0].astype(BF16), ffn_w_out[0].astype(BF16)
    wi = w_in[0]
    win_lat = jnp.concatenate([wi[:, :M_W], wi[:, m_cols:]], axis=1).astype(BF16)
    win_ctx = win_lat[:, :M_W]
    wint_lat = jnp.concatenate([wi[:, M_W:2 * M_W], wi[:, 3 * M_W:m_cols], wi[:, 2 * M_W:3 * M_W]],
                               axis=1).T.astype(BF16)
    wint_ctx = wint_lat[:M_W + GATE_COLS]
    gbias = jnp.broadcast_to(
        jnp.concatenate([i_bias[0].reshape(-1), f_bias[0].reshape(-1)]).reshape(GATE_COLS, 1),
        (GATE_COLS, 128))
    npre, npost = norm_pre[0], norm_post[0]

    tm = min(512, n)
    x1, qk_l, gt_l, vt_l, pool_l, ot_l = _ffn_in(x, mod_all, None, npre, npost, w1s, w2s,
                                                 win_lat, wint_lat, gbias, with_rest=True, tm=tm)
    assert n_ctx % CHUNK == 0
    qk_c, gt_c, vt_c = _ffn_in(ctx.reshape(1, bsz * n_ctx, d), mod_all, bsz, npre, npost, w1s, w2s,
                               win_ctx, wint_ctx, gbias, with_rest=False,
                               tm=min(512, bsz * n_ctx))
    qk_c = qk_c.reshape(bsz, n_ctx, M_W)
    vt_c = vt_c.reshape(bsz, n_ctx // CHUNK, M_W, CHUNK)
    gt_c = gt_c.reshape(GATE_COLS, bsz, n_ctx).transpose(1, 0, 2)

    cb = conv_b[0].reshape(1, M_W)
    wqt = jnp.swapaxes(w_q[0], 1, 2).astype(BF16)
    wkt = jnp.swapaxes(w_k[0], 1, 2).astype(BF16)
    qt_l, kt_l = _qk_proj(qk_l, conv_w[0], cb, wqt, wkt, tq=min(2048, n))
    qt_c, kt_c = _qk_proj(qk_c, conv_w[0], cb, wqt, wkt, tq=min(1024, n_ctx))

    y_pool = _pool_mixer(pool_l, pool_w[0].astype(BF16), pool_scale[0].reshape(1, POOL_W),
                         tt=min(4096, n))

    ct0 = jnp.zeros((bsz * 2 * HEADS, STATE_ROWS, HEAD_DIM), F32)
    m0 = jnp.zeros((bsz, 2 * HEADS, HEAD_DIM), F32)
    ct1, m1 = _mlstm(qt_c, kt_c, vt_c, _gate_scan(gt_c), ct0, m0, emit_h=False)
    ht_f, ht_b = _mlstm(qt_l, kt_l, vt_l, _gate_scan(gt_l), ct1, m1, emit_h=True)

    hnorm = jnp.broadcast_to(head_norm[0].reshape(M_W, 1), (M_W, HEAD_DIM))
    return _mix_ffn(x1, ht_f, ht_b, ot_l, y_pool, mod_all, npre, npost, hnorm,
                    w_out[0].astype(BF16), w1s, w2s, tm=min(1024, n))
```

```python
import functools

import jax
import jax.numpy as jnp
from jax import lax
from jax.experimental import pallas as pl
from jax.experimental.pallas import tpu as pltpu

F32 = jnp.float32
BF16 = jnp.bfloat16

EPS = 1e-6
N_MOD = 9
HEADS = 4
HEAD_DIM = 128
M_W = HEADS * HEAD_DIM
POOL_WINDOWS = (2, 4, 8, 16)
POOL_GW = 128
POOL_W = POOL_GW * len(POOL_WINDOWS)
GRID_W = 64
POOL_PAD = 8
CHUNK = 256
GATE_COLS = 4 * HEADS
FF_CHUNK = 256
STATE_ROWS = HEAD_DIM + 16
MLSTM_CHUNKS_PER_STEP = 4
V7X_VMEM_LIMIT = 56 * 1024 * 1024

_NT = (((1,), (1,)), ((), ()))
_TN = (((0,), (0,)), ((), ()))


def _sigmoid(x):
    return 1.0 / (1.0 + jnp.exp(-x))


def _log_sigmoid(x):
    return jnp.minimum(x, 0.0) - jnp.log(1.0 + jnp.exp(-jnp.abs(x)))


def _unit_rms(x):
    return x * lax.rsqrt(jnp.mean(x * x, axis=-1, keepdims=True) + EPS)


def _ada_in(x, gain, scale, shift):
    return _unit_rms(x) * (gain * (1.0 + scale)) + shift


def _ada_out(x, z, gain, gate):
    return x + _unit_rms(z) * (gain * gate)


def _dot(a, b, dims=None):
    if dims is None:
        return jnp.dot(a, b, preferred_element_type=F32)
    return lax.dot_general(a, b, dims, preferred_element_type=F32)


def _const_spec(shape):
    nd = len(shape)
    return pl.BlockSpec(shape, lambda *_: (0,) * nd, pipeline_mode=pl.Buffered(1))


def _layer_spec(stacked, k):
    nd = stacked.ndim - 1
    return pl.BlockSpec((None,) + stacked.shape[1:], lambda *_: (k,) + (0,) * nd,
                        pipeline_mode=pl.Buffered(1))


def _params(sem):
    return pltpu.CompilerParams(dimension_semantics=sem, vmem_limit_bytes=V7X_VMEM_LIMIT)


def _mod_kernel(s_ref, w_ref, b_ref, o_ref):
    s = s_ref[...]
    act = (s * _sigmoid(s)).astype(BF16)
    o_ref[...] = _dot(act, w_ref[...].astype(BF16)) + b_ref[...]


def _modulation(s_rows, w_mod, b_mod):
    rows, d = s_rows.shape
    n = w_mod.shape[1]
    tn = 1024
    return pl.pallas_call(
        _mod_kernel,
        grid=(n // tn,),
        in_specs=[pl.BlockSpec((rows, d), lambda j: (0, 0)),
                  pl.BlockSpec((d, tn), lambda j: (0, j)),
                  pl.BlockSpec((1, tn), lambda j: (0, j))],
        out_specs=pl.BlockSpec((rows, tn), lambda j: (0, j)),
        out_shape=jax.ShapeDtypeStruct((rows, n), F32),
        compiler_params=_params(("arbitrary",)),
        name="mod",
    )(s_rows, w_mod, b_mod.reshape(1, n))


def _swiglu_into(acc_ref, y_ref, w1_ref, w2_ref):
    y = y_ref[...]
    ff = w2_ref.shape[0]
    for j in range(ff // FF_CHUNK):
        cols = slice(j * FF_CHUNK, (j + 1) * FF_CHUNK)
        a = _dot(y, w1_ref[:, cols])
        b = _dot(y, w1_ref[:, ff + j * FF_CHUNK:ff + (j + 1) * FF_CHUNK])
        h = ((a * _sigmoid(a)) * b).astype(BF16)
        part = _dot(h, w2_ref[cols, :])
        if j == 0:
            acc_ref[...] = part
        else:
            acc_ref[...] += part


def _ffn_in_kernel(with_rest, x_ref, mod_ref, npre_ref, npost_ref, w1_ref, w2_ref,
                   win_ref, wint_ref, gbias_ref, *rest):
    if with_rest:
        x1_ref, qk_ref, gt_ref, vt_ref, pool_ref, ot_ref, y_scr, acc_scr = rest
    else:
        qk_ref, gt_ref, vt_ref, y_scr, acc_scr = rest
    tm = x_ref.shape[0]
    x = x_ref[...]
    y = _ada_in(x, npre_ref[0:1, :], mod_ref[0, 1:2, :], mod_ref[0, 0:1, :])
    y_scr[...] = y.astype(BF16)
    _swiglu_into(acc_scr, y_scr, w1_ref, w2_ref)
    x1 = _ada_out(x, acc_scr[...], npost_ref[0:1, :], 0.5 * mod_ref[0, 2:3, :])
    y2 = _ada_in(x1, npre_ref[1:2, :], mod_ref[0, 4:5, :], mod_ref[0, 3:4, :])
    y_scr[...] = y2.astype(BF16)
    y2b = y_scr[...]
    qk_ref[...] = _dot(y2b, win_ref[:, 0:M_W])
    feat = _dot(wint_ref[...], y2b, _NT)
    g0, g1 = M_W, M_W + GATE_COLS
    gt_ref[...] = feat[g0:g1, :] + jnp.concatenate([gbias_ref[...]] * (tm // 128), axis=1)
    for c in range(vt_ref.shape[0]):
        vt_ref[c] = feat[0:g0, c * CHUNK:(c + 1) * CHUNK].astype(BF16)
    if with_rest:
        x1_ref[...] = x1
        pool_ref[...] = _dot(y2b, win_ref[:, M_W:M_W + POOL_W])
        for c in range(ot_ref.shape[0]):
            ot_ref[c] = feat[g1:g1 + M_W, c * CHUNK:(c + 1) * CHUNK].astype(BF16)


def _ffn_in(x, mod_all, mod_row0, npre, npost, w1s, w2s, win, wint, gbias, *, with_rest, tm):
    bsz, n, d = x.shape
    grid = (bsz, n // tm)
    tok = lambda w: pl.BlockSpec((None, tm, w), lambda b, i: (b, i, 0))
    feat = pl.BlockSpec((None, tm // CHUNK, M_W, CHUNK), lambda b, i: (b, i, 0, 0))
    gates = pl.BlockSpec((None, GATE_COLS, tm), lambda b, i: (b, 0, i))
    if mod_row0 is None:
        mod_spec = pl.BlockSpec((1, N_MOD, d), lambda b, i: (b, 0, 0))
    else:
        mod_spec = pl.BlockSpec((1, N_MOD, d), lambda b, i: (mod_row0, 0, 0))
    in_specs = [tok(d), mod_spec, _const_spec(npre.shape), _const_spec(npost.shape),
                _layer_spec(w1s, 0), _layer_spec(w2s, 0),
                _const_spec(win.shape), _const_spec(wint.shape), _const_spec(gbias.shape)]
    sds = lambda w, dt: jax.ShapeDtypeStruct((bsz, n, w), dt)
    fsds = lambda dt: jax.ShapeDtypeStruct((bsz, n // CHUNK, M_W, CHUNK), dt)
    out_specs = [tok(M_W), gates, feat]
    out_shape = [sds(M_W, F32), jax.ShapeDtypeStruct((bsz, GATE_COLS, n), F32), fsds(BF16)]
    if with_rest:
        out_specs = [tok(d)] + out_specs + [tok(POOL_W), feat]
        out_shape = [sds(d, F32)] + out_shape + [sds(POOL_W, F32), fsds(BF16)]
    return pl.pallas_call(
        functools.partial(_ffn_in_kernel, with_rest),
        grid=grid,
        in_specs=in_specs,
        out_specs=out_specs,
        out_shape=out_shape,
        scratch_shapes=[pltpu.VMEM((tm, d), BF16), pltpu.VMEM((tm, d), F32)],
        compiler_params=_params(("parallel", "arbitrary")),
        name="ffn_in" if with_rest else "ffn_in_ctx",
    )(x, mod_all, npre, npost, w1s, w2s, win, wint, gbias)


def _qk_heads(nt, x_ref, prev_ref, next_ref, cw_ref, cb_ref, wqt_ref, wkt_ref, qt_ref, kt_ref):
    j = pl.program_id(1)
    tq = x_ref.shape[0]
    has_prev = jnp.where(j > 0, 1.0, 0.0)
    has_next = jnp.where(j < nt - 1, 1.0, 0.0)
    row = lax.broadcasted_iota(jnp.int32, (tq, HEAD_DIM), 0)
    scale = HEAD_DIM ** -0.5

    def head(h):
        sl = slice(h * HEAD_DIM, (h + 1) * HEAD_DIM)
        x = x_ref[:, sl]
        up = jnp.where(row == 0, prev_ref[7:8, sl] * has_prev, pltpu.roll(x, 1, 0))
        dn = jnp.where(row == tq - 1, next_ref[0:1, sl] * has_next, pltpu.roll(x, tq - 1, 0))
        conv = cb_ref[:, sl] + up * cw_ref[0:1, sl] + x * cw_ref[1:2, sl] + dn * cw_ref[2:3, sl]
        u = (conv * _sigmoid(conv)).astype(BF16)
        qt = _dot(wqt_ref[h], u, _NT).astype(BF16)
        kt = (_dot(wkt_ref[h], u, _NT) * scale).astype(BF16)
        for c in range(qt_ref.shape[0]):
            qt_ref[c, sl, :] = qt[:, c * CHUNK:(c + 1) * CHUNK]
            kt_ref[c, sl, :] = kt[:, c * CHUNK:(c + 1) * CHUNK]

    return [functools.partial(head, h) for h in range(HEADS)]


def _qk_kernel(*args):
    for head in _qk_heads(*args):
        head()


def _qk_proj(qk_src, conv_w, conv_b, wqt, wkt, *, tq):
    bsz, n, w = qk_src.shape
    nt = n // tq
    blk8 = tq // 8
    last8 = n // 8 - 1
    tok = pl.BlockSpec((None, tq, w), lambda b, j: (b, j, 0))
    prev = pl.BlockSpec((None, 8, w), lambda b, j: (b, jnp.maximum(j * blk8 - 1, 0), 0))
    nxt = pl.BlockSpec((None, 8, w), lambda b, j: (b, jnp.minimum((j + 1) * blk8, last8), 0))
    feat = pl.BlockSpec((None, tq // CHUNK, w, CHUNK), lambda b, j: (b, j, 0, 0))
    out = jax.ShapeDtypeStruct((bsz, n // CHUNK, w, CHUNK), BF16)
    return pl.pallas_call(
        functools.partial(_qk_kernel, nt),
        grid=(bsz, nt),
        in_specs=[tok, prev, nxt, _const_spec(conv_w.shape), _const_spec(conv_b.shape),
                  _const_spec(wqt.shape), _const_spec(wkt.shape)],
        out_specs=[feat, feat],
        out_shape=[out, out],
        compiler_params=_params(("parallel", "arbitrary")),
        name="qk",
    )(qk_src, qk_src, qk_src, conv_w, conv_b, wqt, wkt)


def _pool_groups(nt, grid_h, x_ref, prev_ref, next_ref, wp_ref, ps_ref, y_ref, xs_ref, *bufs):
    j = pl.program_id(1)
    tt = x_ref.shape[0]
    halo = prev_ref.shape[0]
    halo_rows = halo // GRID_W
    rows = tt // GRID_W
    d0, d1 = POOL_PAD, POOL_PAD + GRID_W
    xs_ref[0:halo, :] = prev_ref[...] * jnp.where(j > 0, 1.0, 0.0)
    xs_ref[halo:halo + tt, :] = x_ref[...]
    xs_ref[halo + tt:halo + tt + halo, :] = next_ref[...] * jnp.where(j < nt - 1, 1.0, 0.0)
    zpad = jnp.zeros((rows, POOL_PAD, POOL_GW), F32)
    for buf in bufs:
        buf[:, 0:d0, :] = zpad
        buf[:, d1:d1 + POOL_PAD, :] = zpad
    c = lax.broadcasted_iota(jnp.int32, (1, GRID_W, POOL_GW), 1)
    r = j * rows + lax.broadcasted_iota(jnp.int32, (rows, 1, POOL_GW), 0)

    def shifted_sum(src, dst, k):
        dst[:, d0:d1, :] = src[:, d0:d1, :] + src[:, d0 + k:d1 + k, :]

    def group(g):
        win = POOL_WINDOWS[g]
        lo, hi = win // 2, win - 1 - win // 2
        sl = slice(g * POOL_GW, (g + 1) * POOL_GW)
        first = (halo_rows - lo) * GRID_W
        vs = xs_ref[first:first + (rows + win - 1) * GRID_W, sl]
        span = 1
        while span < win:
            vs = vs[:-span * GRID_W] + vs[span * GRID_W:]
            span *= 2
        a, b, cc, dd = bufs
        a[:, d0:d1, :] = vs.reshape(rows, GRID_W, POOL_GW)
        trail, lead, spare_t, spare_l, k = a, a, b, dd, 1
        while k < lo:
            shifted_sum(trail, spare_t, -k)
            shifted_sum(lead, spare_l, k)
            trail, spare_t = spare_t, (cc if spare_t is b else b)
            lead, spare_l = spare_l, (a if spare_l is dd else dd)
            k *= 2
        tot = trail[:, d0 - 1:d1 - 1, :] + lead[:, d0:d1, :]
        cnt_c = jnp.minimum(c + hi, GRID_W - 1) - jnp.maximum(c - lo, 0) + 1
        cnt_r = jnp.minimum(r + hi, grid_h - 1) - jnp.maximum(r - lo, 0) + 1
        pg = tot * (1.0 / cnt_c.astype(F32)) * (1.0 / cnt_r.astype(F32))
        dlt = (pg.reshape(tt, POOL_GW) - x_ref[:, sl]).astype(BF16)
        y_ref[:, sl] = (_dot(dlt, wp_ref[g]) * ps_ref[:, sl]).astype(BF16)

    return [functools.partial(group, g) for g in range(len(POOL_WINDOWS))]


def _pool_kernel(*args):
    for group in _pool_groups(*args):
        group()


def _pool_mixer(u, wp, ps, *, tt):
    bsz, n, w = u.shape
    nt = n // tt
    halo = 8 * GRID_W
    hb = tt // halo
    last = n // halo - 1
    tok = pl.BlockSpec((None, tt, w), lambda b, j: (b, j, 0))
    prev = pl.BlockSpec((None, halo, w), lambda b, j: (b, jnp.maximum(j * hb - 1, 0), 0))
    nxt = pl.BlockSpec((None, halo, w), lambda b, j: (b, jnp.minimum((j + 1) * hb, last), 0))
    return pl.pallas_call(
        functools.partial(_pool_kernel, nt, n // GRID_W),
        grid=(bsz, nt),
        in_specs=[tok, prev, nxt, _const_spec(wp.shape), _const_spec(ps.shape)],
        out_specs=tok,
        out_shape=jax.ShapeDtypeStruct((bsz, n, w), BF16),
        scratch_shapes=[pltpu.VMEM((tt + 2 * halo, w), F32)]
        + [pltpu.VMEM((tt // GRID_W, GRID_W + 2 * POOL_PAD, POOL_GW), F32)] * 4,
        compiler_params=_params(("parallel", "arbitrary")),
        name="pool",
    )(u, u, u, wp, ps)


def _local_kernel(nt, grid_h, qx_ref, qprev_ref, qnext_ref, cw_ref, cb_ref, wqt_ref, wkt_ref,
                  px_ref, pprev_ref, pnext_ref, wp_ref, ps_ref, qt_ref, kt_ref, y_ref, xs_ref,
                  *bufs):
    heads = _qk_heads(nt, qx_ref, qprev_ref, qnext_ref, cw_ref, cb_ref, wqt_ref, wkt_ref, qt_ref,
                      kt_ref)
    groups = _pool_groups(nt, grid_h, px_ref, pprev_ref, pnext_ref, wp_ref, ps_ref, y_ref, xs_ref,
                          *bufs)
    for head, group in zip(heads, groups):
        head()
        group()


def _local_mixers(qk_src, u, conv_w, conv_b, wqt, wkt, wp, ps, *, tt):
    bsz, n, w = u.shape
    nt = n // tt
    halo = 8 * GRID_W
    tok = pl.BlockSpec((None, tt, w), lambda b, j: (b, j, 0))

    def edges(rows):
        per, last = tt // rows, n // rows - 1
        return (pl.BlockSpec((None, rows, w), lambda b, j: (b, jnp.maximum(j * per - 1, 0), 0)),
                pl.BlockSpec((None, rows, w), lambda b, j: (b, jnp.minimum((j + 1) * per, last), 0)))

    feat = pl.BlockSpec((None, tt // CHUNK, w, CHUNK), lambda b, j: (b, j, 0, 0))
    fsd = jax.ShapeDtypeStruct((bsz, n // CHUNK, w, CHUNK), BF16)
    consts = [conv_w, conv_b, wqt, wkt]
    return pl.pallas_call(
        functools.partial(_local_kernel, nt, n // GRID_W),
        grid=(bsz, nt),
        in_specs=[tok, *edges(8)] + [_const_spec(a.shape) for a in consts]
        + [tok, *edges(halo), _const_spec(wp.shape), _const_spec(ps.shape)],
        out_specs=[feat, feat, tok],
        out_shape=[fsd, fsd, jax.ShapeDtypeStruct((bsz, n, w), BF16)],
        scratch_shapes=[pltpu.VMEM((tt + 2 * halo, w), F32)]
        + [pltpu.VMEM((tt // GRID_W, GRID_W + 2 * POOL_PAD, POOL_GW), F32)] * 4,
        compiler_params=_params(("parallel", "arbitrary")),
        name="local",
    )(qk_src, qk_src, qk_src, *consts, u, u, u, wp, ps)


def _split3(x):
    hi = x.astype(BF16).astype(F32)
    r = x - hi
    mid = r.astype(BF16).astype(F32)
    lo = (r - mid).astype(BF16).astype(F32)
    return hi, mid, lo


def _gate_scan_kernel(chunk, gt_ref, o_ref):
    n_combo = 2 * HEADS
    g = gt_ref[...]
    n = g.shape[1]
    pos = jnp.bitwise_and(lax.broadcasted_iota(jnp.int32, (n_combo, n), 1), chunk - 1)
    is_fwd_row = lax.broadcasted_iota(jnp.int32, (n_combo, n), 0) < HEADS

    def scan(x, op, ident):
        sft = 1
        while sft < chunk:
            from_l = jnp.where(pos >= sft, pltpu.roll(x, sft, 1), ident)
            from_r = jnp.where(pos < chunk - sft, pltpu.roll(x, n - sft, 1), ident)
            x = op(x, jnp.where(is_fwd_row, from_l, from_r))
            sft *= 2
        return x

    brow = scan(_log_sigmoid(g[n_combo:]), jnp.add, 0.0)
    rowb = g[:n_combo] - brow
    o_ref[0:n_combo, :] = rowb
    o_ref[n_combo:2 * n_combo, :] = brow
    o_ref[2 * n_combo:, :] = scan(rowb, jnp.maximum, -jnp.inf)


def _gate_scan(gt):
    bsz, rows, n = gt.shape
    return pl.pallas_call(
        functools.partial(_gate_scan_kernel, CHUNK),
        grid=(bsz,),
        in_specs=[pl.BlockSpec((None, rows, n), lambda b: (b, 0, 0))],
        out_specs=pl.BlockSpec((None, 3 * rows // 2, n), lambda b: (b, 0, 0)),
        out_shape=jax.ShapeDtypeStruct((bsz, 3 * rows // 2, n), F32),
        compiler_params=_params(("parallel",)),
        name="gate_scan",
    )(gt)


def _mlstm_kernel(emit_h, emit_state, *refs):
    fwd, bwd = refs[0:4], refs[4:8]
    ct0_ref, m0_ref = refs[8:10]
    outs = list(refs[10:])
    hf_ref, hb_ref = (outs.pop(0), outs.pop(0)) if emit_h else (None, None)
    cto_ref, mo_ref = (outs.pop(0), outs.pop(0)) if emit_state else (None, None)
    ct_scr, m_scr = outs
    i = pl.program_id(0)
    n_combo = 2 * HEADS
    bsz, cps, _, L = fwd[0].shape

    @pl.when(i == 0)
    def _():
        ct_scr[...] = ct0_ref[...]
        m_scr[...] = m0_ref[...]

    is_fwd_row = lax.broadcasted_iota(jnp.int32, (n_combo, L), 0) < HEADS
    is_fwd_col = lax.broadcasted_iota(jnp.int32, (n_combo, 1), 0) < HEADS
    krow = lax.broadcasted_iota(jnp.int32, (8, L), 0)
    tpos = lax.broadcasted_iota(jnp.int32, (L, L), 0)
    spos = lax.broadcasted_iota(jnp.int32, (L, L), 1)
    masks = (spos <= tpos, spos >= tpos)
    ones_rows = jnp.ones((STATE_ROWS - HEAD_DIM, L), BF16)

    def position_rows(b, slots):
        lanes = [slice(u * L, (u + 1) * L) for u in slots]
        pick = lambda r: jnp.where(is_fwd_row, fwd[3][b, r * n_combo:(r + 1) * n_combo, lanes[0]],
                                   bwd[3][b, r * n_combo:(r + 1) * n_combo, lanes[1]])
        last = lambda a: jnp.where(is_fwd_col, a[:, L - 1:L], a[:, 0:1])
        rowb, brow, cmax = pick(0), pick(1), pick(2)
        total, rowb_max = last(brow), last(cmax)
        m_in = m_scr[b][:, 0:1]
        mm = jnp.maximum(cmax, m_in)
        m_loc = total + rowb_max
        m_new = jnp.maximum(total + m_in, m_loc)
        m_scr[b] = jnp.broadcast_to(m_new, (n_combo, HEAD_DIM))
        return dict(w_inter=jnp.exp(m_in - mm), floor=jnp.exp(-(brow + mm)),
                    w_state=jnp.exp(rowb - rowb_max), a_old=jnp.exp(total + m_in - m_new),
                    a_new=jnp.exp(m_loc - m_new), rb3=_split3(rowb), nm3=_split3(-mm))

    def front(b, d, h, u, pr):
        c = d * HEADS + h
        sl = slice(h * HEAD_DIM, (h + 1) * HEAD_DIM)
        row = lambda a: a[c:c + 1, :]
        src = fwd if d == 0 else bwd
        qt, kt, vt = src[0][b, u, sl, :], src[1][b, u, sl, :], src[2][b, u, sl, :]
        vext = jnp.concatenate([vt, ones_rows], axis=0)
        ct = ct_scr[b * n_combo + c]
        out = dict(b=b, d=d, u=u, sl=sl, vext=vext, w_inter=row(pr["w_inter"]),
                   floor=row(pr["floor"]))
        if emit_h:
            nm3, rb3 = pr["nm3"], pr["rb3"]
            lhs = jnp.where(krow < 3, 1.0, jnp.where(krow == 3, row(nm3[0]), jnp.where(
                krow == 4, row(nm3[1]), jnp.where(krow == 5, row(nm3[2]), 0.0))))
            rhs = jnp.where(krow == 0, row(rb3[0]), jnp.where(krow == 1, row(rb3[1]), jnp.where(
                krow == 2, row(rb3[2]), jnp.where(krow < 6, 1.0, 0.0))))
            out["expo"] = _dot(lhs.astype(BF16), rhs.astype(BF16), _TN)
            out["s0"] = _dot(qt, kt, _TN)
            out["state_t"] = _dot(ct.astype(BF16), qt)
        ktw = (kt.astype(F32) * row(pr["w_state"])).astype(BF16)
        ckv = _dot(vext, ktw, _NT)
        ct_scr[b * n_combo + c] = pr["a_old"][c:c + 1, :] * ct + pr["a_new"][c:c + 1, :] * ckv
        return out

    def back(f):
        dmat = jnp.where(masks[f["d"]], jnp.exp(f["expo"]), 0.0)
        s = (f["s0"] * dmat).astype(BF16)
        res = f["w_inter"] * f["state_t"] + _dot(f["vext"], s, _NT)
        den = res[HEAD_DIM:HEAD_DIM + 1, :]
        h_ref = hf_ref if f["d"] == 0 else hb_ref
        hval = res[:HEAD_DIM] / jnp.maximum(jnp.abs(den), f["floor"])
        h_ref[f["b"], f["u"], f["sl"], :] = hval.astype(h_ref.dtype)

    pending = None
    for step in range(cps):
        slots = (step, cps - 1 - step)
        for b in range(bsz):
            pr = position_rows(b, slots)
            for d in range(2):
                for h in range(HEADS):
                    cur = front(b, d, h, slots[d], pr)
                    if emit_h and pending is not None:
                        back(pending)
                    pending = cur
    if emit_h:
        back(pending)

    if emit_state:
        @pl.when(i == pl.num_programs(0) - 1)
        def _():
            cto_ref[...] = ct_scr[...]
            mo_ref[...] = m_scr[...]


def _mlstm(qt, kt, vt, sc, ct0, m0, *, emit_h):
    bsz, nc, w, _ = qt.shape
    cps = MLSTM_CHUNKS_PER_STEP if nc % MLSTM_CHUNKS_PER_STEP == 0 else 1
    steps = nc // cps
    fpos = lambda i: i
    bpos = lambda i: steps - 1 - i

    def feat(pos_fn):
        return pl.BlockSpec((bsz, cps, w, CHUNK), lambda i: (0, pos_fn(i), 0, 0))

    def specs(pos_fn):
        scs = pl.BlockSpec((bsz, sc.shape[1], cps * CHUNK), lambda i: (0, 0, pos_fn(i)))
        return [feat(pos_fn)] * 3 + [scs]

    ct_spec = pl.BlockSpec(ct0.shape, lambda i: (0, 0, 0))
    m_spec = pl.BlockSpec(m0.shape, lambda i: (0, 0, 0))
    if emit_h:
        hsd = jax.ShapeDtypeStruct((bsz, nc, w, CHUNK), BF16)
        out_specs = [feat(fpos), feat(bpos)]
        out_shape = [hsd, hsd]
    else:
        out_specs = [ct_spec, m_spec]
        out_shape = [jax.ShapeDtypeStruct(ct0.shape, F32), jax.ShapeDtypeStruct(m0.shape, F32)]
    return pl.pallas_call(
        functools.partial(_mlstm_kernel, emit_h, not emit_h),
        grid=(steps,),
        in_specs=specs(fpos) + specs(bpos) + [ct_spec, m_spec],
        out_specs=out_specs,
        out_shape=out_shape,
        scratch_shapes=[pltpu.VMEM(ct0.shape, F32), pltpu.VMEM(m0.shape, F32)],
        compiler_params=_params(("arbitrary",)),
        name="mlstm" if emit_h else "mlstm_ctx",
    )(qt, kt, vt, sc, qt, kt, vt, sc, ct0, m0)


def _mix_ffn_kernel(x1_ref, hf_ref, hb_ref, ot_ref, yp_ref, mod_ref, npre_ref, npost_ref, hn_ref,
                    wout_ref, w1_ref, w2_ref, out_ref, mixt_scr, y_scr, acc_scr):
    hnorm = jnp.concatenate([hn_ref[...]] * (CHUNK // HEAD_DIM), axis=1)
    for c in range(hf_ref.shape[0]):
        for h in range(HEADS):
            sl = slice(h * HEAD_DIM, (h + 1) * HEAD_DIM)
            hh = hf_ref[c, sl, :].astype(F32) + hb_ref[c, sl, :].astype(F32)
            inv = lax.rsqrt(jnp.mean(hh * hh, axis=0, keepdims=True) + EPS)
            gated = hh * inv * hnorm[sl, :] * _sigmoid(ot_ref[c, sl, :].astype(F32))
            mixt_scr[sl, c * CHUNK:(c + 1) * CHUNK] = gated.astype(BF16)
    z = _dot(mixt_scr[...], wout_ref[0:M_W, :], _TN) + _dot(yp_ref[...], wout_ref[M_W:, :])
    x2 = _ada_out(x1_ref[...], z, npost_ref[1:2, :], 1.0 * mod_ref[0, 5:6, :])
    y = _ada_in(x2, npre_ref[2:3, :], mod_ref[0, 7:8, :], mod_ref[0, 6:7, :])
    y_scr[...] = y.astype(BF16)
    _swiglu_into(acc_scr, y_scr, w1_ref, w2_ref)
    out_ref[...] = _ada_out(x2, acc_scr[...], npost_ref[2:3, :], 0.5 * mod_ref[0, 8:9, :])


def _mix_ffn(x1, hf, hb, ot, yp, mod_all, npre, npost, hnorm, wout, w1s, w2s, *, tm):
    bsz, n, d = x1.shape
    tok = lambda w: pl.BlockSpec((None, tm, w), lambda b, i: (b, i, 0))
    feat = pl.BlockSpec((None, tm // CHUNK, M_W, CHUNK), lambda b, i: (b, i, 0, 0))
    in_specs = [tok(d), feat, feat, feat, tok(POOL_W),
                pl.BlockSpec((1, N_MOD, d), lambda b, i: (b, 0, 0)),
                _const_spec(npre.shape), _const_spec(npost.shape), _const_spec(hnorm.shape),
                _const_spec(wout.shape), _layer_spec(w1s, 1), _layer_spec(w2s, 1)]
    return pl.pallas_call(
        _mix_ffn_kernel,
        grid=(bsz, n // tm),
        in_specs=in_specs,
        out_specs=tok(d),
        out_shape=jax.ShapeDtypeStruct((bsz, n, d), F32),
        scratch_shapes=[pltpu.VMEM((M_W, tm), BF16), pltpu.VMEM((tm, d), BF16),
                        pltpu.VMEM((tm, d), F32)],
        compiler_params=_params(("parallel", "arbitrary")),
        name="mix_ffn",
    )(x1, hf, hb, ot, yp, mod_all, npre, npost, hnorm, wout, w1s, w2s)


def kernel(x, c, ctx, c_ctx, w_mod, b_mod, norm_pre, norm_post, ffn_w_in, ffn_w_out, w_in, w_out,
           conv_w, conv_b, w_q, w_k, i_bias, f_bias, head_norm, pool_w, pool_scale):
    assert w_mod.shape[0] == 1, "single-layer stack"
    bsz, n, d = x.shape
    n_ctx = ctx.shape[1]
    m_cols = 3 * M_W + GATE_COLS

    rows = 8
    s_rows = jnp.zeros((rows, d), F32).at[:bsz].set(c).at[bsz].set(c_ctx)
    mod_all = _modulation(s_rows, w_mod[0], b_mod[0]).reshape(rows, N_MOD, d)

    w1s, w2s = ffn_w_in[0].astype(BF16), ffn_w_out[0].astype(BF16)
    wi = w_in[0]
    win_lat = jnp.concatenate([wi[:, :M_W], wi[:, m_cols:]], axis=1).astype(BF16)
    win_ctx = win_lat[:, :M_W]
    wint_lat = jnp.concatenate([wi[:, M_W:2 * M_W], wi[:, 3 * M_W:m_cols], wi[:, 2 * M_W:3 * M_W]],
                               axis=1).T.astype(BF16)
    wint_ctx = wint_lat[:M_W + GATE_COLS]
    gbias = jnp.broadcast_to(
        jnp.concatenate([i_bias[0].reshape(-1), f_bias[0].reshape(-1)]).reshape(GATE_COLS, 1),
        (GATE_COLS, 128))
    npre, npost = norm_pre[0], norm_post[0]

    tm = min(512, n)
    x1, qk_l, gt_l, vt_l, pool_l, ot_l = _ffn_in(x, mod_all, None, npre, npost, w1s, w2s,
                                                 win_lat, wint_lat, gbias, with_rest=True, tm=tm)
    assert n_ctx % CHUNK == 0
    qk_c, gt_c, vt_c = _ffn_in(ctx.reshape(1, bsz * n_ctx, d), mod_all, bsz, npre, npost, w1s, w2s,
                               win_ctx, wint_ctx, gbias, with_rest=False,
                               tm=min(512, bsz * n_ctx))
    qk_c = qk_c.reshape(bsz, n_ctx, M_W)
    vt_c = vt_c.reshape(bsz, n_ctx // CHUNK, M_W, CHUNK)
    gt_c = gt_c.reshape(GATE_COLS, bsz, n_ctx).transpose(1, 0, 2)

    cb = conv_b[0].reshape(1, M_W)
    wqt = jnp.swapaxes(w_q[0], 1, 2).astype(BF16)
    wkt = jnp.swapaxes(w_k[0], 1, 2).astype(BF16)
    qt_c, kt_c = _qk_proj(qk_c, conv_w[0], cb, wqt, wkt, tq=min(1024, n_ctx))
    qt_l, kt_l, y_pool = _local_mixers(qk_l, pool_l, conv_w[0], cb, wqt, wkt,
                                       pool_w[0].astype(BF16), pool_scale[0].reshape(1, POOL_W),
                                       tt=min(2048, n))

    ct0 = jnp.zeros((bsz * 2 * HEADS, STATE_ROWS, HEAD_DIM), F32)
    m0 = jnp.zeros((bsz, 2 * HEADS, HEAD_DIM), F32)
    ct1, m1 = _mlstm(qt_c, kt_c, vt_c, _gate_scan(gt_c), ct0, m0, emit_h=False)
    ht_f, ht_b = _mlstm(qt_l, kt_l, vt_l, _gate_scan(gt_l), ct1, m1, emit_h=True)

    hnorm = jnp.broadcast_to(head_norm[0].reshape(M_W, 1), (M_W, HEAD_DIM))
    return _mix_ffn(x1, ht_f, ht_b, ot_l, y_pool, mod_all, npre, npost, hnorm,
                    w_out[0].astype(BF16), w1s, w2s, tm=min(1024, n))
```

```python
import functools

import jax
import jax.numpy as jnp
from jax import lax
from jax.experimental import pallas as pl
from jax.experimental.pallas import tpu as pltpu

F32 = jnp.float32
BF16 = jnp.bfloat16

EPS = 1e-6
N_MOD = 9
HEADS = 4
HEAD_DIM = 128
M_W = HEADS * HEAD_DIM
POOL_WINDOWS = (2, 4, 8, 16)
POOL_GW = 128
POOL_W = POOL_GW * len(POOL_WINDOWS)
GRID_W = 64
QK_EDGE = 16
POOL_PAD = 8
CHUNK = 256
GATE_COLS = 4 * HEADS
FF_CHUNK = 256
STATE_ROWS = HEAD_DIM + 16
MLSTM_CHUNKS_PER_STEP = 4
V7X_VMEM_LIMIT = 56 * 1024 * 1024

_NT = (((1,), (1,)), ((), ()))
_TN = (((0,), (0,)), ((), ()))


def _sigmoid(x):
    return 1.0 / (1.0 + jnp.exp(-x))


def _log_sigmoid(x):
    return jnp.minimum(x, 0.0) - jnp.log(1.0 + jnp.exp(-jnp.abs(x)))


def _unit_rms(x):
    return x * lax.rsqrt(jnp.mean(x * x, axis=-1, keepdims=True) + EPS)


def _ada_in(x, gain, scale, shift):
    return _unit_rms(x) * (gain * (1.0 + scale)) + shift


def _ada_out(x, z, gain, gate):
    return x + _unit_rms(z) * (gain * gate)


def _dot(a, b, dims=None):
    if dims is None:
        return jnp.dot(a, b, preferred_element_type=F32)
    return lax.dot_general(a, b, dims, preferred_element_type=F32)


def _const_spec(shape):
    nd = len(shape)
    return pl.BlockSpec(shape, lambda *_: (0,) * nd, pipeline_mode=pl.Buffered(1))


def _layer_spec(stacked, k):
    nd = stacked.ndim - 1
    return pl.BlockSpec((None,) + stacked.shape[1:], lambda *_: (k,) + (0,) * nd,
                        pipeline_mode=pl.Buffered(1))


def _params(sem):
    return pltpu.CompilerParams(dimension_semantics=sem, vmem_limit_bytes=V7X_VMEM_LIMIT)


def _mod_kernel(s_ref, w_ref, b_ref, o_ref):
    s = s_ref[...]
    act = (s * _sigmoid(s)).astype(BF16)
    o_ref[...] = _dot(act, w_ref[...].astype(BF16)) + b_ref[...]


def _modulation(s_rows, w_mod, b_mod):
    rows, d = s_rows.shape
    n = w_mod.shape[1]
    tn = 1024
    return pl.pallas_call(
        _mod_kernel,
        grid=(n // tn,),
        in_specs=[pl.BlockSpec((rows, d), lambda j: (0, 0)),
                  pl.BlockSpec((d, tn), lambda j: (0, j)),
                  pl.BlockSpec((1, tn), lambda j: (0, j))],
        out_specs=pl.BlockSpec((rows, tn), lambda j: (0, j)),
        out_shape=jax.ShapeDtypeStruct((rows, n), F32),
        compiler_params=_params(("arbitrary",)),
        name="mod",
    )(s_rows, w_mod, b_mod.reshape(1, n))


def _swiglu_into(acc_ref, y_ref, w1_ref, w2_ref):
    y = y_ref[...]
    ff = w2_ref.shape[0]
    for j in range(ff // FF_CHUNK):
        cols = slice(j * FF_CHUNK, (j + 1) * FF_CHUNK)
        a = _dot(y, w1_ref[:, cols])
        b = _dot(y, w1_ref[:, ff + j * FF_CHUNK:ff + (j + 1) * FF_CHUNK])
        h = ((a * _sigmoid(a)) * b).astype(BF16)
        part = _dot(h, w2_ref[cols, :])
        if j == 0:
            acc_ref[...] = part
        else:
            acc_ref[...] += part


def _ffn_in_kernel(with_rest, x_ref, mod_ref, npre_ref, npost_ref, w1_ref, w2_ref,
                   win_ref, wint_ref, gbias_ref, *rest):
    if with_rest:
        x1_ref, qk_ref, gt_ref, vt_ref, pool_ref, ot_ref, y_scr, acc_scr = rest
    else:
        qk_ref, gt_ref, vt_ref, y_scr, acc_scr = rest
    tm = x_ref.shape[0]
    x = x_ref[...]
    y = _ada_in(x, npre_ref[0:1, :], mod_ref[0, 1:2, :], mod_ref[0, 0:1, :])
    y_scr[...] = y.astype(BF16)
    _swiglu_into(acc_scr, y_scr, w1_ref, w2_ref)
    x1 = _ada_out(x, acc_scr[...], npost_ref[0:1, :], 0.5 * mod_ref[0, 2:3, :])
    y2 = _ada_in(x1, npre_ref[1:2, :], mod_ref[0, 4:5, :], mod_ref[0, 3:4, :])
    y_scr[...] = y2.astype(BF16)
    y2b = y_scr[...]
    qk_ref[...] = _dot(y2b, win_ref[:, 0:M_W]).astype(BF16)
    feat = _dot(wint_ref[...], y2b, _NT)
    g0, g1 = M_W, M_W + GATE_COLS
    gt_ref[...] = feat[g0:g1, :] + jnp.concatenate([gbias_ref[...]] * (tm // 128), axis=1)
    for c in range(vt_ref.shape[0]):
        vt_ref[c] = feat[0:g0, c * CHUNK:(c + 1) * CHUNK].astype(BF16)
    if with_rest:
        x1_ref[...] = x1
        pool_ref[...] = _dot(y2b, win_ref[:, M_W:M_W + POOL_W])
        for c in range(ot_ref.shape[0]):
            ot_ref[c] = feat[g1:g1 + M_W, c * CHUNK:(c + 1) * CHUNK].astype(BF16)


def _ffn_in(x, mod_all, mod_row0, npre, npost, w1s, w2s, win, wint, gbias, *, with_rest, tm):
    bsz, n, d = x.shape
    grid = (bsz, n // tm)
    tok = lambda w: pl.BlockSpec((None, tm, w), lambda b, i: (b, i, 0))
    feat = pl.BlockSpec((None, tm // CHUNK, M_W, CHUNK), lambda b, i: (b, i, 0, 0))
    gates = pl.BlockSpec((None, GATE_COLS, tm), lambda b, i: (b, 0, i))
    if mod_row0 is None:
        mod_spec = pl.BlockSpec((1, N_MOD, d), lambda b, i: (b, 0, 0))
    else:
        mod_spec = pl.BlockSpec((1, N_MOD, d), lambda b, i: (mod_row0, 0, 0))
    in_specs = [tok(d), mod_spec, _const_spec(npre.shape), _const_spec(npost.shape),
                _layer_spec(w1s, 0), _layer_spec(w2s, 0),
                _const_spec(win.shape), _const_spec(wint.shape), _const_spec(gbias.shape)]
    sds = lambda w, dt: jax.ShapeDtypeStruct((bsz, n, w), dt)
    fsds = lambda dt: jax.ShapeDtypeStruct((bsz, n // CHUNK, M_W, CHUNK), dt)
    out_specs = [tok(M_W), gates, feat]
    out_shape = [sds(M_W, BF16), jax.ShapeDtypeStruct((bsz, GATE_COLS, n), F32), fsds(BF16)]
    if with_rest:
        out_specs = [tok(d)] + out_specs + [tok(POOL_W), feat]
        out_shape = [sds(d, F32)] + out_shape + [sds(POOL_W, F32), fsds(BF16)]
    return pl.pallas_call(
        functools.partial(_ffn_in_kernel, with_rest),
        grid=grid,
        in_specs=in_specs,
        out_specs=out_specs,
        out_shape=out_shape,
        scratch_shapes=[pltpu.VMEM((tm, d), BF16), pltpu.VMEM((tm, d), F32)],
        compiler_params=_params(("parallel", "arbitrary")),
        name="ffn_in" if with_rest else "ffn_in_ctx",
    )(x, mod_all, npre, npost, w1s, w2s, win, wint, gbias)


def _qk_heads(nt, x_ref, prev_ref, next_ref, cw_ref, cb_ref, wqt_ref, wkt_ref, qt_ref, kt_ref):
    j = pl.program_id(1)
    tq = x_ref.shape[0]
    has_prev = jnp.where(j > 0, 1.0, 0.0)
    has_next = jnp.where(j < nt - 1, 1.0, 0.0)
    row = lax.broadcasted_iota(jnp.int32, (tq, HEAD_DIM), 0)
    scale = HEAD_DIM ** -0.5

    def head(h):
        sl = slice(h * HEAD_DIM, (h + 1) * HEAD_DIM)
        x = x_ref[:, sl].astype(F32)
        before = prev_ref[QK_EDGE - 1:QK_EDGE, sl].astype(F32) * has_prev
        after = next_ref[0:1, sl].astype(F32) * has_next
        up = jnp.where(row == 0, before, pltpu.roll(x, 1, 0))
        dn = jnp.where(row == tq - 1, after, pltpu.roll(x, tq - 1, 0))
        conv = cb_ref[:, sl] + up * cw_ref[0:1, sl] + x * cw_ref[1:2, sl] + dn * cw_ref[2:3, sl]
        u = (conv * _sigmoid(conv)).astype(BF16)
        qt = _dot(wqt_ref[h], u, _NT).astype(BF16)
        kt = (_dot(wkt_ref[h], u, _NT) * scale).astype(BF16)
        for c in range(qt_ref.shape[0]):
            qt_ref[c, sl, :] = qt[:, c * CHUNK:(c + 1) * CHUNK]
            kt_ref[c, sl, :] = kt[:, c * CHUNK:(c + 1) * CHUNK]

    return [functools.partial(head, h) for h in range(HEADS)]


def _qk_kernel(*args):
    for head in _qk_heads(*args):
        head()


def _qk_proj(qk_src, conv_w, conv_b, wqt, wkt, *, tq):
    bsz, n, w = qk_src.shape
    nt = n // tq
    per, last = tq // QK_EDGE, n // QK_EDGE - 1
    tok = pl.BlockSpec((None, tq, w), lambda b, j: (b, j, 0))
    prev = pl.BlockSpec((None, QK_EDGE, w), lambda b, j: (b, jnp.maximum(j * per - 1, 0), 0))
    nxt = pl.BlockSpec((None, QK_EDGE, w), lambda b, j: (b, jnp.minimum((j + 1) * per, last), 0))
    feat = pl.BlockSpec((None, tq // CHUNK, w, CHUNK), lambda b, j: (b, j, 0, 0))
    out = jax.ShapeDtypeStruct((bsz, n // CHUNK, w, CHUNK), BF16)
    return pl.pallas_call(
        functools.partial(_qk_kernel, nt),
        grid=(bsz, nt),
        in_specs=[tok, prev, nxt, _const_spec(conv_w.shape), _const_spec(conv_b.shape),
                  _const_spec(wqt.shape), _const_spec(wkt.shape)],
        out_specs=[feat, feat],
        out_shape=[out, out],
        compiler_params=_params(("parallel", "arbitrary")),
        name="qk",
    )(qk_src, qk_src, qk_src, conv_w, conv_b, wqt, wkt)


def _pool_groups(nt, grid_h, x_ref, prev_ref, next_ref, wp_ref, ps_ref, y_ref, xs_ref, *bufs):
    j = pl.program_id(1)
    tt = x_ref.shape[0]
    halo = prev_ref.shape[0]
    halo_rows = halo // GRID_W
    rows = tt // GRID_W
    d0, d1 = POOL_PAD, POOL_PAD + GRID_W
    xs_ref[0:halo, :] = prev_ref[...] * jnp.where(j > 0, 1.0, 0.0)
    xs_ref[halo:halo + tt, :] = x_ref[...]
    xs_ref[halo + tt:halo + tt + halo, :] = next_ref[...] * jnp.where(j < nt - 1, 1.0, 0.0)
    zpad = jnp.zeros((rows, POOL_PAD, POOL_GW), F32)
    for buf in bufs:
        buf[:, 0:d0, :] = zpad
        buf[:, d1:d1 + POOL_PAD, :] = zpad
    c = lax.broadcasted_iota(jnp.int32, (1, GRID_W, POOL_GW), 1)
    r = j * rows + lax.broadcasted_iota(jnp.int32, (rows, 1, POOL_GW), 0)

    def shifted_sum(src, dst, k):
        dst[:, d0:d1, :] = src[:, d0:d1, :] + src[:, d0 + k:d1 + k, :]

    def group(g):
        win = POOL_WINDOWS[g]
        lo, hi = win // 2, win - 1 - win // 2
        sl = slice(g * POOL_GW, (g + 1) * POOL_GW)
        first = (halo_rows - lo) * GRID_W
        vs = xs_ref[first:first + (rows + win - 1) * GRID_W, sl]
        span = 1
        while span < win:
            vs = vs[:-span * GRID_W] + vs[span * GRID_W:]
            span *= 2
        a, b, cc, dd = bufs
        a[:, d0:d1, :] = vs.reshape(rows, GRID_W, POOL_GW)
        trail, lead, spare_t, spare_l, k = a, a, b, dd, 1
        while k < lo:
            shifted_sum(trail, spare_t, -k)
            shifted_sum(lead, spare_l, k)
            trail, spare_t = spare_t, (cc if spare_t is b else b)
            lead, spare_l = spare_l, (a if spare_l is dd else dd)
            k *= 2
        tot = trail[:, d0 - 1:d1 - 1, :] + lead[:, d0:d1, :]
        cnt_c = jnp.minimum(c + hi, GRID_W - 1) - jnp.maximum(c - lo, 0) + 1
        cnt_r = jnp.minimum(r + hi, grid_h - 1) - jnp.maximum(r - lo, 0) + 1
        pg = tot * (1.0 / cnt_c.astype(F32)) * (1.0 / cnt_r.astype(F32))
        dlt = (pg.reshape(tt, POOL_GW) - x_ref[:, sl]).astype(BF16)
        y_ref[:, sl] = (_dot(dlt, wp_ref[g]) * ps_ref[:, sl]).astype(BF16)

    return [functools.partial(group, g) for g in range(len(POOL_WINDOWS))]


def _pool_kernel(*args):
    for group in _pool_groups(*args):
        group()


def _pool_mixer(u, wp, ps, *, tt):
    bsz, n, w = u.shape
    nt = n // tt
    halo = 8 * GRID_W
    hb = tt // halo
    last = n // halo - 1
    tok = pl.BlockSpec((None, tt, w), lambda b, j: (b, j, 0))
    prev = pl.BlockSpec((None, halo, w), lambda b, j: (b, jnp.maximum(j * hb - 1, 0), 0))
    nxt = pl.BlockSpec((None, halo, w), lambda b, j: (b, jnp.minimum((j + 1) * hb, last), 0))
    return pl.pallas_call(
        functools.partial(_pool_kernel, nt, n // GRID_W),
        grid=(bsz, nt),
        in_specs=[tok, prev, nxt, _const_spec(wp.shape), _const_spec(ps.shape)],
        out_specs=tok,
        out_shape=jax.ShapeDtypeStruct((bsz, n, w), BF16),
        scratch_shapes=[pltpu.VMEM((tt + 2 * halo, w), F32)]
        + [pltpu.VMEM((tt // GRID_W, GRID_W + 2 * POOL_PAD, POOL_GW), F32)] * 4,
        compiler_params=_params(("parallel", "arbitrary")),
        name="pool",
    )(u, u, u, wp, ps)


def _local_kernel(nt, grid_h, qx_ref, qprev_ref, qnext_ref, cw_ref, cb_ref, wqt_ref, wkt_ref,
                  px_ref, pprev_ref, pnext_ref, wp_ref, ps_ref, qt_ref, kt_ref, y_ref, xs_ref,
                  *bufs):
    heads = _qk_heads(nt, qx_ref, qprev_ref, qnext_ref, cw_ref, cb_ref, wqt_ref, wkt_ref, qt_ref,
                      kt_ref)
    groups = _pool_groups(nt, grid_h, px_ref, pprev_ref, pnext_ref, wp_ref, ps_ref, y_ref, xs_ref,
                          *bufs)
    for head, group in zip(heads, groups):
        head()
        group()


def _local_mixers(qk_src, u, conv_w, conv_b, wqt, wkt, wp, ps, *, tt):
    bsz, n, w = u.shape
    nt = n // tt
    halo = 8 * GRID_W
    tok = pl.BlockSpec((None, tt, w), lambda b, j: (b, j, 0))

    def edges(rows):
        per, last = tt // rows, n // rows - 1
        return (pl.BlockSpec((None, rows, w), lambda b, j: (b, jnp.maximum(j * per - 1, 0), 0)),
                pl.BlockSpec((None, rows, w), lambda b, j: (b, jnp.minimum((j + 1) * per, last), 0)))

    feat = pl.BlockSpec((None, tt // CHUNK, w, CHUNK), lambda b, j: (b, j, 0, 0))
    fsd = jax.ShapeDtypeStruct((bsz, n // CHUNK, w, CHUNK), BF16)
    consts = [conv_w, conv_b, wqt, wkt]
    return pl.pallas_call(
        functools.partial(_local_kernel, nt, n // GRID_W),
        grid=(bsz, nt),
        in_specs=[tok, *edges(QK_EDGE)] + [_const_spec(a.shape) for a in consts]
        + [tok, *edges(halo), _const_spec(wp.shape), _const_spec(ps.shape)],
        out_specs=[feat, feat, tok],
        out_shape=[fsd, fsd, jax.ShapeDtypeStruct((bsz, n, w), BF16)],
        scratch_shapes=[pltpu.VMEM((tt + 2 * halo, w), F32)]
        + [pltpu.VMEM((tt // GRID_W, GRID_W + 2 * POOL_PAD, POOL_GW), F32)] * 4,
        compiler_params=_params(("parallel", "arbitrary")),
        name="local",
    )(qk_src, qk_src, qk_src, *consts, u, u, u, wp, ps)


def _split3(x):
    hi = x.astype(BF16).astype(F32)
    r = x - hi
    mid = r.astype(BF16).astype(F32)
    lo = (r - mid).astype(BF16).astype(F32)
    return hi, mid, lo


def _gate_scan_kernel(chunk, gt_ref, o_ref):
    n_combo = 2 * HEADS
    g = gt_ref[...]
    n = g.shape[1]
    pos = jnp.bitwise_and(lax.broadcasted_iota(jnp.int32, (n_combo, n), 1), chunk - 1)
    is_fwd_row = lax.broadcasted_iota(jnp.int32, (n_combo, n), 0) < HEADS

    def scan(x, op, ident):
        sft = 1
        while sft < chunk:
            from_l = jnp.where(pos >= sft, pltpu.roll(x, sft, 1), ident)
            from_r = jnp.where(pos < chunk - sft, pltpu.roll(x, n - sft, 1), ident)
            x = op(x, jnp.where(is_fwd_row, from_l, from_r))
            sft *= 2
        return x

    brow = scan(_log_sigmoid(g[n_combo:]), jnp.add, 0.0)
    rowb = g[:n_combo] - brow
    o_ref[0:n_combo, :] = rowb
    o_ref[n_combo:2 * n_combo, :] = brow
    o_ref[2 * n_combo:, :] = scan(rowb, jnp.maximum, -jnp.inf)


def _gate_scan(gt):
    bsz, rows, n = gt.shape
    return pl.pallas_call(
        functools.partial(_gate_scan_kernel, CHUNK),
        grid=(bsz,),
        in_specs=[pl.BlockSpec((None, rows, n), lambda b: (b, 0, 0))],
        out_specs=pl.BlockSpec((None, 3 * rows // 2, n), lambda b: (b, 0, 0)),
        out_shape=jax.ShapeDtypeStruct((bsz, 3 * rows // 2, n), F32),
        compiler_params=_params(("parallel",)),
        name="gate_scan",
    )(gt)


def _mlstm_kernel(emit_h, emit_state, *refs):
    fwd, bwd = refs[0:4], refs[4:8]
    ct0_ref, m0_ref = refs[8:10]
    outs = list(refs[10:])
    hf_ref, hb_ref = (outs.pop(0), outs.pop(0)) if emit_h else (None, None)
    cto_ref, mo_ref = (outs.pop(0), outs.pop(0)) if emit_state else (None, None)
    ct_scr, m_scr = outs
    i = pl.program_id(0)
    n_combo = 2 * HEADS
    bsz, cps, _, L = fwd[0].shape

    @pl.when(i == 0)
    def _():
        ct_scr[...] = ct0_ref[...]
        m_scr[...] = m0_ref[...]

    is_fwd_row = lax.broadcasted_iota(jnp.int32, (n_combo, L), 0) < HEADS
    is_fwd_col = lax.broadcasted_iota(jnp.int32, (n_combo, 1), 0) < HEADS
    krow = lax.broadcasted_iota(jnp.int32, (8, L), 0)
    tpos = lax.broadcasted_iota(jnp.int32, (L, L), 0)
    spos = lax.broadcasted_iota(jnp.int32, (L, L), 1)
    masks = (spos <= tpos, spos >= tpos)
    ones_rows = jnp.ones((STATE_ROWS - HEAD_DIM, L), BF16)

    def position_rows(b, slots):
        lanes = [slice(u * L, (u + 1) * L) for u in slots]
        pick = lambda r: jnp.where(is_fwd_row, fwd[3][b, r * n_combo:(r + 1) * n_combo, lanes[0]],
                                   bwd[3][b, r * n_combo:(r + 1) * n_combo, lanes[1]])
        last = lambda a: jnp.where(is_fwd_col, a[:, L - 1:L], a[:, 0:1])
        rowb, brow, cmax = pick(0), pick(1), pick(2)
        total, rowb_max = last(brow), last(cmax)
        m_in = m_scr[b][:, 0:1]
        mm = jnp.maximum(cmax, m_in)
        m_loc = total + rowb_max
        m_new = jnp.maximum(total + m_in, m_loc)
        m_scr[b] = jnp.broadcast_to(m_new, (n_combo, HEAD_DIM))
        return dict(w_inter=jnp.exp(m_in - mm), floor=jnp.exp(-(brow + mm)),
                    w_state=jnp.exp(rowb - rowb_max), a_old=jnp.exp(total + m_in - m_new),
                    a_new=jnp.exp(m_loc - m_new), rb3=_split3(rowb), nm3=_split3(-mm))

    def front(b, d, h, u, pr):
        c = d * HEADS + h
        sl = slice(h * HEAD_DIM, (h + 1) * HEAD_DIM)
        row = lambda a: a[c:c + 1, :]
        src = fwd if d == 0 else bwd
        qt, kt, vt = src[0][b, u, sl, :], src[1][b, u, sl, :], src[2][b, u, sl, :]
        vext = jnp.concatenate([vt, ones_rows], axis=0)
        ct = ct_scr[b * n_combo + c]
        out = dict(b=b, d=d, u=u, sl=sl, vext=vext, w_inter=row(pr["w_inter"]),
                   floor=row(pr["floor"]))
        if emit_h:
            nm3, rb3 = pr["nm3"], pr["rb3"]
            lhs = jnp.where(krow < 3, 1.0, jnp.where(krow == 3, row(nm3[0]), jnp.where(
                krow == 4, row(nm3[1]), jnp.where(krow == 5, row(nm3[2]), 0.0))))
            rhs = jnp.where(krow == 0, row(rb3[0]), jnp.where(krow == 1, row(rb3[1]), jnp.where(
                krow == 2, row(rb3[2]), jnp.where(krow < 6, 1.0, 0.0))))
            out["expo"] = _dot(lhs.astype(BF16), rhs.astype(BF16), _TN)
            out["s0"] = _dot(qt, kt, _TN)
            out["state_t"] = _dot(ct.astype(BF16), qt)
        ktw = (kt.astype(F32) * row(pr["w_state"])).astype(BF16)
        ckv = _dot(vext, ktw, _NT)
        ct_scr[b * n_combo + c] = pr["a_old"][c:c + 1, :] * ct + pr["a_new"][c:c + 1, :] * ckv
        return out

    def back(f):
        dmat = jnp.where(masks[f["d"]], jnp.exp(f["expo"]), 0.0)
        s = (f["s0"] * dmat).astype(BF16)
        res = f["w_inter"] * f["state_t"] + _dot(f["vext"], s, _NT)
        den = res[HEAD_DIM:HEAD_DIM + 1, :]
        h_ref = hf_ref if f["d"] == 0 else hb_ref
        hval = res[:HEAD_DIM] / jnp.maximum(jnp.abs(den), f["floor"])
        h_ref[f["b"], f["u"], f["sl"], :] = hval.astype(h_ref.dtype)

    pending = None
    for step in range(cps):
        slots = (step, cps - 1 - step)
        for b in range(bsz):
            pr = position_rows(b, slots)
            for d in range(2):
                for h in range(HEADS):
                    cur = front(b, d, h, slots[d], pr)
                    if emit_h and pending is not None:
                        back(pending)
                    pending = cur
    if emit_h:
        back(pending)

    if emit_state:
        @pl.when(i == pl.num_programs(0) - 1)
        def _():
            cto_ref[...] = ct_scr[...]
            mo_ref[...] = m_scr[...]


def _mlstm(qt, kt, vt, sc, ct0, m0, *, emit_h):
    bsz, nc, w, _ = qt.shape
    cps = MLSTM_CHUNKS_PER_STEP if nc % MLSTM_CHUNKS_PER_STEP == 0 else 1
    steps = nc // cps
    fpos = lambda i: i
    bpos = lambda i: steps - 1 - i

    def feat(pos_fn):
        return pl.BlockSpec((bsz, cps, w, CHUNK), lambda i: (0, pos_fn(i), 0, 0))

    def specs(pos_fn):
        scs = pl.BlockSpec((bsz, sc.shape[1], cps * CHUNK), lambda i: (0, 0, pos_fn(i)))
        return [feat(pos_fn)] * 3 + [scs]

    ct_spec = pl.BlockSpec(ct0.shape, lambda i: (0, 0, 0))
    m_spec = pl.BlockSpec(m0.shape, lambda i: (0, 0, 0))
    if emit_h:
        hsd = jax.ShapeDtypeStruct((bsz, nc, w, CHUNK), BF16)
        out_specs = [feat(fpos), feat(bpos)]
        out_shape = [hsd, hsd]
    else:
        out_specs = [ct_spec, m_spec]
        out_shape = [jax.ShapeDtypeStruct(ct0.shape, F32), jax.ShapeDtypeStruct(m0.shape, F32)]
    return pl.pallas_call(
        functools.partial(_mlstm_kernel, emit_h, not emit_h),
        grid=(steps,),
        in_specs=specs(fpos) + specs(bpos) + [ct_spec, m_spec],
        out_specs=out_specs,
        out_shape=out_shape,
        scratch_shapes=[pltpu.VMEM(ct0.shape, F32), pltpu.VMEM(m0.shape, F32)],
        compiler_params=_params(("arbitrary",)),
        name="mlstm" if emit_h else "mlstm_ctx",
    )(qt, kt, vt, sc, qt, kt, vt, sc, ct0, m0)


def _mix_ffn_kernel(x1_ref, hf_ref, hb_ref, ot_ref, yp_ref, mod_ref, npre_ref, npost_ref, hn_ref,
                    wout_ref, w1_ref, w2_ref, out_ref, mixt_scr, y_scr, acc_scr):
    hnorm = jnp.concatenate([hn_ref[...]] * (CHUNK // HEAD_DIM), axis=1)
    for c in range(hf_ref.shape[0]):
        for h in range(HEADS):
            sl = slice(h * HEAD_DIM, (h + 1) * HEAD_DIM)
            hh = hf_ref[c, sl, :].astype(F32) + hb_ref[c, sl, :].astype(F32)
            inv = lax.rsqrt(jnp.mean(hh * hh, axis=0, keepdims=True) + EPS)
            gated = hh * inv * hnorm[sl, :] * _sigmoid(ot_ref[c, sl, :].astype(F32))
            mixt_scr[sl, c * CHUNK:(c + 1) * CHUNK] = gated.astype(BF16)
    z = _dot(mixt_scr[...], wout_ref[0:M_W, :], _TN) + _dot(yp_ref[...], wout_ref[M_W:, :])
    x2 = _ada_out(x1_ref[...], z, npost_ref[1:2, :], 1.0 * mod_ref[0, 5:6, :])
    y = _ada_in(x2, npre_ref[2:3, :], mod_ref[0, 7:8, :], mod_ref[0, 6:7, :])
    y_scr[...] = y.astype(BF16)
    _swiglu_into(acc_scr, y_scr, w1_ref, w2_ref)
    out_ref[...] = _ada_out(x2, acc_scr[...], npost_ref[2:3, :], 0.5 * mod_ref[0, 8:9, :])


def _mix_ffn(x1, hf, hb, ot, yp, mod_all, npre, npost, hnorm, wout, w1s, w2s, *, tm):
    bsz, n, d = x1.shape
    tok = lambda w: pl.BlockSpec((None, tm, w), lambda b, i: (b, i, 0))
    feat = pl.BlockSpec((None, tm // CHUNK, M_W, CHUNK), lambda b, i: (b, i, 0, 0))
    in_specs = [tok(d), feat, feat, feat, tok(POOL_W),
                pl.BlockSpec((1, N_MOD, d), lambda b, i: (b, 0, 0)),
                _const_spec(npre.shape), _const_spec(npost.shape), _const_spec(hnorm.shape),
                _const_spec(wout.shape), _layer_spec(w1s, 1), _layer_spec(w2s, 1)]
    return pl.pallas_call(
        _mix_ffn_kernel,
        grid=(bsz, n // tm),
        in_specs=in_specs,
        out_specs=tok(d),
        out_shape=jax.ShapeDtypeStruct((bsz, n, d), F32),
        scratch_shapes=[pltpu.VMEM((M_W, tm), BF16), pltpu.VMEM((tm, d), BF16),
                        pltpu.VMEM((tm, d), F32)],
        compiler_params=_params(("parallel", "arbitrary")),
        name="mix_ffn",
    )(x1, hf, hb, ot, yp, mod_all, npre, npost, hnorm, wout, w1s, w2s)


def kernel(x, c, ctx, c_ctx, w_mod, b_mod, norm_pre, norm_post, ffn_w_in, ffn_w_out, w_in, w_out,
           conv_w, conv_b, w_q, w_k, i_bias, f_bias, head_norm, pool_w, pool_scale):
    assert w_mod.shape[0] == 1, "single-layer stack"
    bsz, n, d = x.shape
    n_ctx = ctx.shape[1]
    m_cols = 3 * M_W + GATE_COLS

    rows = 8
    s_rows = jnp.zeros((rows, d), F32).at[:bsz].set(c).at[bsz].set(c_ctx)
    mod_all = _modulation(s_rows, w_mod[0], b_mod[0]).reshape(rows, N_MOD, d)

    w1s, w2s = ffn_w_in[0].astype(BF16), ffn_w_out[0].astype(BF16)
    wi = w_in[0]
    win_lat = jnp.concatenate([wi[:, :M_W], wi[:, m_cols:]], axis=1).astype(BF16)
    win_ctx = win_lat[:, :M_W]
    wint_lat = jnp.concatenate([wi[:, M_W:2 * M_W], wi[:, 3 * M_W:m_cols], wi[:, 2 * M_W:3 * M_W]],
                               axis=1).T.astype(BF16)
    wint_ctx = wint_lat[:M_W + GATE_COLS]
    gbias = jnp.broadcast_to(
        jnp.concatenate([i_bias[0].reshape(-1), f_bias[0].reshape(-1)]).reshape(GATE_COLS, 1),
        (GATE_COLS, 128))
    npre, npost = norm_pre[0], norm_post[0]

    tm = min(512, n)
    x1, qk_l, gt_l, vt_l, pool_l, ot_l = _ffn_in(x, mod_all, None, npre, npost, w1s, w2s,
                                                 win_lat, wint_lat, gbias, with_rest=True, tm=tm)
    assert n_ctx % CHUNK == 0
    qk_c, gt_c, vt_c = _ffn_in(ctx.reshape(1, bsz * n_ctx, d), mod_all, bsz, npre, npost, w1s, w2s,
                               win_ctx, wint_ctx, gbias, with_rest=False,
                               tm=min(512, bsz * n_ctx))
    qk_c = qk_c.reshape(bsz, n_ctx, M_W)
    vt_c = vt_c.reshape(bsz, n_ctx // CHUNK, M_W, CHUNK)
    gt_c = gt_c.reshape(GATE_COLS, bsz, n_ctx).transpose(1, 0, 2)

    cb = conv_b[0].reshape(1, M_W)
    wqt = jnp.swapaxes(w_q[0], 1, 2).astype(BF16)
    wkt = jnp.swapaxes(w_k[0], 1, 2).astype(BF16)
    qt_c, kt_c = _qk_proj(qk_c, conv_w[0], cb, wqt, wkt, tq=min(1024, n_ctx))
    qt_l, kt_l, y_pool = _local_mixers(qk_l, pool_l, conv_w[0], cb, wqt, wkt,
                                       pool_w[0].astype(BF16), pool_scale[0].reshape(1, POOL_W),
                                       tt=min(2048, n))

    ct0 = jnp.zeros((bsz * 2 * HEADS, STATE_ROWS, HEAD_DIM), F32)
    m0 = jnp.zeros((bsz, 2 * HEADS, HEAD_DIM), F32)
    ct1, m1 = _mlstm(qt_c, kt_c, vt_c, _gate_scan(gt_c), ct0, m0, emit_h=False)
    ht_f, ht_b = _mlstm(qt_l, kt_l, vt_l, _gate_scan(gt_l), ct1, m1, emit_h=True)

    hnorm = jnp.broadcast_to(head_norm[0].reshape(M_W, 1), (M_W, HEAD_DIM))
    return _mix_ffn(x1, ht_f, ht_b, ot_l, y_pool, mod_all, npre, npost, hnorm,
                    w_out[0].astype(BF16), w1s, w2s, tm=min(1024, n))
```
